```python
import jax, jax.numpy as jnp
from jax import lax
import numpy as np

D_MODEL = 2048
BATCH = 2
SEQ = 8192
DEPTH = 4

N_MEM = 256
N_A_LAYERS = DEPTH // 2
N_B_LAYERS = DEPTH - N_A_LAYERS
EPS = 1e-6

MEM_HEADS = 4
MEM_WIDTH = D_MODEL // 4
MEM_HEAD_DIM = MEM_WIDTH // MEM_HEADS
MAIN_WIDTH = D_MODEL - MEM_WIDTH

MLSTM_HEADS = 4
MLSTM_V_DIM = MAIN_WIDTH // MLSTM_HEADS
MLSTM_QK_DIM = MLSTM_V_DIM // 2
MLSTM_CHUNK = 64
CONV_WIDTH = 4

MLA_NOPE_DIM = 128
MLA_ROPE_DIM = 64
MLA_QK_DIM = MLA_NOPE_DIM + MLA_ROPE_DIM
MLA_V_DIM = 128
MLA_HEADS = MAIN_WIDTH // MLA_V_DIM
Q_LORA_RANK = 448
KV_LORA_RANK = 512
ROPE_THETA = 10000.0
Q_BLOCK = 128

D_FF = 5632

A_COLS = (2 * MLSTM_HEADS * MLSTM_QK_DIM, MAIN_WIDTH, MAIN_WIDTH, 2 * MLSTM_HEADS, MEM_WIDTH)
B_COLS = (Q_LORA_RANK, MEM_WIDTH)

F32 = jnp.float32

kernel_name = 'yoco_mlstm_mla_macaron_memory'


def _split(t, sizes):
    idx = [int(i) for i in np.cumsum(sizes)[:-1]]
    return jnp.split(t, idx, axis=-1)


def _rms_norm(t, gain):
    tf = t.astype(F32)
    y = tf * lax.rsqrt(jnp.mean(tf * tf, axis=-1, keepdims=True) + EPS)
    return (y * gain.astype(F32)).astype(t.dtype)


def _swiglu(h, w_in, w_out):
    g, u = jnp.split(h @ w_in, 2, axis=-1)
    return (jax.nn.silu(g) * u) @ w_out


def _rope_tables(positions):
    inv_freq = ROPE_THETA ** (-jnp.arange(0, MLA_ROPE_DIM, 2, dtype=F32) / MLA_ROPE_DIM)
    ang = positions.astype(F32)[..., None] * inv_freq
    return jnp.cos(ang), jnp.sin(ang)


def _rope(t, cos, sin):
    x1, x2 = jnp.split(t.astype(F32), 2, axis=-1)
    c = cos[:, :, None, :]
    s = sin[:, :, None, :]
    return jnp.concatenate([x1 * c - x2 * s, x2 * c + x1 * s], axis=-1).astype(t.dtype)


def _causal_conv(t, w):
    return lax.conv_general_dilated(t, w[:, None, :], window_strides=(1,), padding=[(CONV_WIDTH - 1, 0)],
                                    dimension_numbers=('NWC', 'WIO', 'NWC'), feature_group_count=t.shape[-1])


def _to_chunks(t):
    b, s, h = t.shape[:3]
    t = t.reshape((b, s // MLSTM_CHUNK, MLSTM_CHUNK, h) + t.shape[3:])
    return jnp.moveaxis(t, (1, 3), (0, 2))


def _from_chunks(t):
    t = jnp.moveaxis(t, (0, 2), (1, 3))
    b, nc, l, h, d = t.shape
    return t.reshape(b, nc * l, h, d)


def _mlstm_chunk_step(carry, xs):
    c_state, n_state, m_state = carry
    q, k, v, ig, lf = xs
    L = q.shape[2]
    causal = jnp.tril(jnp.ones((L, L), dtype=bool))
    b = jnp.cumsum(lf, axis=-1)
    log_w = jnp.where(causal, b[..., :, None] - b[..., None, :] + ig[..., None, :], -jnp.inf)
    log_inter = b + m_state[..., None]
    m_t = jnp.maximum(jnp.max(log_w, axis=-1), log_inter)
    w = jnp.exp(log_w - m_t[..., None])
    w_inter = jnp.exp(log_inter - m_t)
    s = jnp.einsum('bhtd,bhsd->bhts', q, k) * w
    num = jnp.einsum('bhts,bhsv->bhtv', s, v) + w_inter[..., None] * jnp.einsum('bhtd,bhdv->bhtv', q, c_state)
    den = jnp.sum(s, axis=-1) + w_inter * jnp.einsum('bhtd,bhd->bht', q, n_state)
    h = num / jnp.maximum(jnp.abs(den), jnp.exp(-m_t))[..., None]
    g = b[..., -1]
    log_a = g[..., None] - b + ig
    m_new = jnp.maximum(g + m_state, jnp.max(log_a, axis=-1))
    decay = jnp.exp(g + m_state - m_new)
    a = jnp.exp(log_a - m_new[..., None])
    c_new = decay[..., None, None] * c_state + jnp.einsum('bhs,bhsd,bhsv->bhdv', a, k, v)
    n_new = decay[..., None] * n_state + jnp.einsum('bhs,bhsd->bhd', a, k)
    return (c_new, n_new, m_new), h


def _mlstm_branch(h, w_in, b_gates, conv_w, head_gain):
    bsz, seq, _ = h.shape
    qk, v, o, gates, mq = _split(h @ w_in, A_COLS)
    qk = jax.nn.silu(_causal_conv(qk, conv_w))
    q, k = jnp.split(qk, 2, axis=-1)
    q = q.reshape(bsz, seq, MLSTM_HEADS, MLSTM_QK_DIM).astype(F32) * (MLSTM_QK_DIM ** -0.5)
    k = k.reshape(bsz, seq, MLSTM_HEADS, MLSTM_QK_DIM).astype(F32)
    v = v.reshape(bsz, seq, MLSTM_HEADS, MLSTM_V_DIM).astype(F32)
    gates = gates.astype(F32) + b_gates.astype(F32)
    ig, fg = jnp.split(gates, 2, axis=-1)
    lf = jax.nn.log_sigmoid(fg)
    init = (jnp.zeros((bsz, MLSTM_HEADS, MLSTM_QK_DIM, MLSTM_V_DIM), F32),
            jnp.zeros((bsz, MLSTM_HEADS, MLSTM_QK_DIM), F32),
            jnp.zeros((bsz, MLSTM_HEADS), F32))
    xs = (_to_chunks(q), _to_chunks(k), _to_chunks(v), _to_chunks(ig), _to_chunks(lf))
    _, hs = lax.scan(_mlstm_chunk_step, init, xs)
    hs = _rms_norm(_from_chunks(hs), head_gain.reshape(MLSTM_HEADS, MLSTM_V_DIM))
    out = jax.nn.sigmoid(o.astype(F32)) * hs.reshape(bsz, seq, MAIN_WIDTH)
    return out.astype(h.dtype), mq


def _causal_block_attention(q, k, v):
    bsz, seq, nh, dqk = q.shape
    nblk = seq // Q_BLOCK
    scale = dqk ** -0.5
    qb = jnp.moveaxis(q.reshape(bsz, nblk, Q_BLOCK, nh, dqk), 1, 0)
    starts = jnp.arange(nblk, dtype=jnp.int32) * Q_BLOCK
    k_pos = jnp.arange(seq, dtype=jnp.int32)

    def one_block(args):
        q_blk, start = args
        s = jnp.einsum('bqhd,bkhd->bhqk', q_blk, k).astype(F32) * scale
        q_pos = start + jnp.arange(Q_BLOCK, dtype=jnp.int32)
        s = jnp.where(k_pos[None, :] <= q_pos[:, None], s, -jnp.inf)
        p = jax.nn.softmax(s, axis=-1).astype(v.dtype)
        return jnp.einsum('bhqk,bkhd->bqhd', p, v)

    out = lax.map(one_block, (qb, starts))
    return jnp.moveaxis(out, 0, 1).reshape(bsz, seq, nh, v.shape[-1])


def _shared_kv(x, kv_gain, w_dkv, kv_latent_gain, w_ukv, k_gain, cos, sin):
    bsz, seq, _ = x.shape
    c_kv, k_pe = jnp.split(_rms_norm(x, kv_gain) @ w_dkv, [KV_LORA_RANK], axis=-1)
    kv = (_rms_norm(c_kv, kv_latent_gain) @ w_ukv).reshape(bsz, seq, MLA_HEADS, MLA_NOPE_DIM + MLA_V_DIM)
    k_nope, v = jnp.split(kv, [MLA_NOPE_DIM], axis=-1)
    k_pe = jnp.broadcast_to(k_pe[:, :, None, :], (bsz, seq, MLA_HEADS, MLA_ROPE_DIM))
    k = _rms_norm(jnp.concatenate([k_nope, k_pe], axis=-1), k_gain)
    k = jnp.concatenate([k[..., :MLA_NOPE_DIM], _rope(k[..., MLA_NOPE_DIM:], cos, sin)], axis=-1)
    return k, v


def _mla_branch(h, w_in, q_latent_gain, w_uq, q_gain, k_sh, v_sh, cos, sin):
    bsz, seq, _ = h.shape
    cq, mq = _split(h @ w_in, B_COLS)
    q = (_rms_norm(cq, q_latent_gain) @ w_uq).reshape(bsz, seq, MLA_HEADS, MLA_QK_DIM)
    q = _rms_norm(q, q_gain)
    q = jnp.concatenate([q[..., :MLA_NOPE_DIM], _rope(q[..., MLA_NOPE_DIM:], cos, sin)], axis=-1)
    o = _causal_block_attention(q, k_sh, v_sh)
    return o.reshape(bsz, seq, MAIN_WIDTH), mq


def _memory_kv(mem, mem_gain, w_mem_kv, k_gain):
    bsz, nm, _ = mem.shape
    mk, mv = jnp.split(_rms_norm(mem, mem_gain) @ w_mem_kv, 2, axis=-1)
    mk = _rms_norm(mk.reshape(bsz, nm, MEM_HEADS, MEM_HEAD_DIM), k_gain)
    mv = mv.reshape(bsz, nm, MEM_HEADS, MEM_HEAD_DIM)
    return mk, mv


def _memory_attention(mq, mk, mv, q_gain):
    bsz, seq, _ = mq.shape
    q = _rms_norm(mq.reshape(bsz, seq, MEM_HEADS, MEM_HEAD_DIM), q_gain)
    s = jnp.einsum('bshd,bmhd->bhsm', q, mk).astype(F32) * (MEM_HEAD_DIM ** -0.5)
    p = jax.nn.softmax(s, axis=-1).astype(mv.dtype)
    return jnp.einsum('bhsm,bmhd->bshd', p, mv).reshape(bsz, seq, MEM_WIDTH)


def setup_inputs(seed: int = 0) -> dict:
    key = jax.random.key(seed)
    k = jax.random.split(key, 32)

    def w(i, shape, fan_in):
        return jax.random.normal(k[i], shape, F32) * (fan_in ** -0.5)

    def gain(i, shape):
        return 1.0 + 0.02 * jax.random.normal(k[i], shape, F32)

    a_cols = int(sum(A_COLS))
    b_cols = int(sum(B_COLS))
    offsets = jax.random.randint(k[2], (BATCH, 1), 0, 1024, dtype=jnp.int32)
    positions = offsets + jnp.arange(SEQ, dtype=jnp.int32)[None, :]
    i_bias = 0.1 * jax.random.normal(k[13], (N_A_LAYERS, MLSTM_HEADS), F32)
    f_bias = jnp.linspace(3.0, 6.0, MLSTM_HEADS, dtype=F32)[None, :] + 0.1 * jax.random.normal(k[14], (N_A_LAYERS, MLSTM_HEADS), F32)
    return {
        'x': jax.random.normal(k[0], (BATCH, SEQ, D_MODEL), F32),
        'mem': jax.random.normal(k[1], (BATCH, N_MEM, D_MODEL), F32),
        'positions': positions,
        'ffn1_gain': gain(3, (DEPTH, D_MODEL)),
        'ffn1_w_in': w(4, (DEPTH, D_MODEL, 2 * D_FF), D_MODEL),
        'ffn1_w_out': w(5, (DEPTH, D_FF, D_MODEL), D_FF),
        'mix_gain': gain(6, (DEPTH, D_MODEL)),
        'w_out': w(7, (DEPTH, D_MODEL, D_MODEL), D_MODEL),
        'mem_gain': gain(8, (DEPTH, D_MODEL)),
        'w_mem_kv': w(9, (DEPTH, D_MODEL, 2 * MEM_WIDTH), D_MODEL),
        'mem_q_gain': gain(10, (DEPTH, MEM_HEAD_DIM)),
        'mem_k_gain': gain(11, (DEPTH, MEM_HEAD_DIM)),
        'a_w_in': w(12, (N_A_LAYERS, D_MODEL, a_cols), D_MODEL),
        'a_b_gates': jnp.concatenate([i_bias, f_bias], axis=-1),
        'a_conv': w(15, (N_A_LAYERS, CONV_WIDTH, 2 * MLSTM_HEADS * MLSTM_QK_DIM), CONV_WIDTH),
        'a_head_gain': gain(16, (N_A_LAYERS, MAIN_WIDTH)),
        'kv_gain': gain(17, (D_MODEL,)),
        'w_dkv': w(18, (D_MODEL, KV_LORA_RANK + MLA_ROPE_DIM), D_MODEL),
        'kv_latent_gain': gain(19, (KV_LORA_RANK,)),
        'w_ukv': w(20, (KV_LORA_RANK, MLA_HEADS * (MLA_NOPE_DIM + MLA_V_DIM)), KV_LORA_RANK),
        'k_gain': gain(21, (MLA_QK_DIM,)),
        'b_w_in': w(22, (N_B_LAYERS, D_MODEL, b_cols), D_MODEL),
        'b_q_latent_gain': gain(23, (N_B_LAYERS, Q_LORA_RANK)),
        'b_w_uq': w(24, (N_B_LAYERS, Q_LORA_RANK, MLA_HEADS * MLA_QK_DIM), Q_LORA_RANK),
        'b_q_gain': gain(25, (N_B_LAYERS, MLA_QK_DIM)),
        'ffn2_gain': gain(26, (DEPTH, D_MODEL)),
        'ffn2_w_in': w(27, (DEPTH, D_MODEL, 2 * D_FF), D_MODEL),
        'ffn2_w_out': w(28, (DEPTH, D_FF, D_MODEL), D_FF),
    }


def reference(x, mem, positions, ffn1_gain, ffn1_w_in, ffn1_w_out, mix_gain, w_out, mem_gain, w_mem_kv,
              mem_q_gain, mem_k_gain, a_w_in, a_b_gates, a_conv, a_head_gain, kv_gain, w_dkv, kv_latent_gain,
              w_ukv, k_gain, b_w_in, b_q_latent_gain, b_w_uq, b_q_gain, ffn2_gain, ffn2_w_in, ffn2_w_out):
    cos, sin = _rope_tables(positions)
    k_sh = None
    v_sh = None
    for layer in range(DEPTH):
        x = x + 0.5 * _swiglu(_rms_norm(x, ffn1_gain[layer]), ffn1_w_in[layer], ffn1_w_out[layer])
        h = _rms_norm(x, mix_gain[layer])
        if layer < N_A_LAYERS:
            a = layer
            main, mq = _mlstm_branch(h, a_w_in[a], a_b_gates[a], a_conv[a], a_head_gain[a])
        else:
            j = layer - N_A_LAYERS
            main, mq = _mla_branch(h, b_w_in[j], b_q_latent_gain[j], b_w_uq[j], b_q_gain[j], k_sh, v_sh, cos, sin)
        mk, mv = _memory_kv(mem, mem_gain[layer], w_mem_kv[layer], mem_k_gain[layer])
        mem_out = _memory_attention(mq, mk, mv, mem_q_gain[layer])
        x = x + jnp.concatenate([main, mem_out], axis=-1) @ w_out[layer]
        x = x + 0.5 * _swiglu(_rms_norm(x, ffn2_gain[layer]), ffn2_w_in[layer], ffn2_w_out[layer])
        if layer == N_A_LAYERS - 1:
            k_sh, v_sh = _shared_kv(x, kv_gain, w_dkv, kv_latent_gain, w_ukv, k_gain, cos, sin)
    return x
```

```python
import functools

import jax
import jax.numpy as jnp
from jax import lax
from jax.experimental import pallas as pl
from jax.experimental.pallas import tpu as pltpu

F32 = jnp.float32
BF16 = jnp.bfloat16

D_MODEL = 2048
DEPTH = 4
N_MEM = 256
N_A_LAYERS = DEPTH // 2
EPS = 1e-6
MEM_HEADS = 4
MEM_WIDTH = D_MODEL // 4
MEM_HEAD_DIM = MEM_WIDTH // MEM_HEADS
MAIN_WIDTH = D_MODEL - MEM_WIDTH
MLSTM_HEADS = 4
MLSTM_V_DIM = MAIN_WIDTH // MLSTM_HEADS
MLSTM_QK_DIM = MLSTM_V_DIM // 2
CONV_WIDTH = 4
MLA_NOPE_DIM = 128
MLA_ROPE_DIM = 64
MLA_QK_DIM = MLA_NOPE_DIM + MLA_ROPE_DIM
MLA_V_DIM = 128
MLA_HEADS = MAIN_WIDTH // MLA_V_DIM
Q_LORA_RANK = 448
KV_LORA_RANK = 512
ROPE_THETA = 10000.0
D_FF = 5632

LANES = 128
SUBLANES = 8
V7X_VMEM_BYTES = 64 * 1024 * 1024

QK_PAD = 256
Q_LORA_PAD = 512
HEAD_PAD = 256
GATE_PAD = LANES
KV_DOWN_PAD = KV_LORA_RANK + LANES

TM = 512
TF = 512
CHUNK = 256
TQ = 512
TROPE = 2048
VMEM_LIMIT = 56 * 1024 * 1024


def _params(sem):
    return pltpu.CompilerParams(dimension_semantics=sem, vmem_limit_bytes=VMEM_LIMIT)


def _rms(x, gain, n=None):
    if n is None:
        ms = jnp.mean(x * x, axis=-1, keepdims=True)
    else:
        ms = jnp.sum(x * x, axis=-1, keepdims=True) * (1.0 / n)
    return (x * lax.rsqrt(ms + EPS)) * gain


def _log_sigmoid(x):
    return -(jnp.maximum(-x, 0.0) + jnp.log1p(jnp.exp(-jnp.abs(x))))


def _dot(a, b):
    return jnp.dot(a, b, preferred_element_type=F32)


def _dot_nt(a, b):
    return lax.dot_general(a, b, (((1,), (1,)), ((), ())), preferred_element_type=F32)


def _dot_tn(a, b):
    return lax.dot_general(a, b, (((0,), (0,)), ((), ())), preferred_element_type=F32)


def _ffn_body(emit_next, x_ref, g_ref, wg_ref, wu_ref, wo_ref, *rest):
    if emit_next:
        ng_ref, o_ref, hn_ref, h_scr = rest
    else:
        o_ref, h_scr = rest
    j = pl.program_id(1)

    @pl.when(j == 0)
    def _():
        x = x_ref[...]
        h_scr[...] = _rms(x, g_ref[...]).astype(BF16)
        o_ref[...] = x

    h = h_scr[...]
    g = _dot(h, wg_ref[...])
    u = _dot(h, wu_ref[...])
    a = ((g * jax.nn.sigmoid(g)) * u) * 0.5
    o_ref[...] += _dot(a.astype(BF16), wo_ref[...])

    if emit_next:
        @pl.when(j == pl.num_programs(1) - 1)
        def _():
            hn_ref[...] = _rms(o_ref[...], ng_ref[...]).astype(BF16)


def _ffn(x, gain, w_in, w_out, layer, next_gain=None):
    t, d = x.shape
    nf = D_FF // TF
    emit = next_gain is not None
    in_specs = [
        pl.BlockSpec((TM, d), lambda i, j: (i, 0)),
        pl.BlockSpec((1, d), lambda i, j: (0, 0)),
        pl.BlockSpec((None, d, TF), lambda i, j: (layer, 0, j)),
        pl.BlockSpec((None, d, TF), lambda i, j: (layer, 0, j + nf)),
        pl.BlockSpec((None, TF, d), lambda i, j: (layer, j, 0)),
    ]
    args = [x, gain.reshape(1, d), w_in, w_in, w_out]
    out_shape = [jax.ShapeDtypeStruct((t, d), F32)]
    out_specs = [pl.BlockSpec((TM, d), lambda i, j: (i, 0))]
    if emit:
        in_specs.append(pl.BlockSpec((1, d), lambda i, j: (0, 0)))
        args.append(next_gain.reshape(1, d))
        out_shape.append(jax.ShapeDtypeStruct((t, d), BF16))
        out_specs.append(pl.BlockSpec((TM, d), lambda i, j: (i, 0)))
    res = pl.pallas_call(
        functools.partial(_ffn_body, emit),
        grid=(t // TM, nf),
        in_specs=in_specs,
        out_specs=out_specs,
        out_shape=out_shape,
        scratch_shapes=[pltpu.VMEM((TM, d), BF16)],
        compiler_params=_params(("parallel", "arbitrary")),
        name="ffn_next" if emit else "ffn",
    )(*args)
    return (res[0], res[1]) if emit else res[0]


def _mm_body(a_ref, w_ref, o_ref):
    o_ref[...] = _dot(a_ref[...], w_ref[...]).astype(o_ref.dtype)


def _matmul(a, w, out_dtype, tn):
    t, k = a.shape
    n = w.shape[1]
    return pl.pallas_call(
        _mm_body,
        grid=(t // TM, n // tn),
        in_specs=[pl.BlockSpec((TM, k), lambda i, j: (i, 0)),
                  pl.BlockSpec((k, tn), lambda i, j: (0, j))],
        out_specs=pl.BlockSpec((TM, tn), lambda i, j: (i, j)),
        out_shape=jax.ShapeDtypeStruct((t, n), out_dtype),
        compiler_params=_params(("parallel", "arbitrary")),
        name="proj",
    )(a, w)


def _gates_body(h_ref, w_ref, wt_ref, brow_ref, bcol_ref, g_ref, gt_ref):
    h = h_ref[...]
    g = _dot(h, w_ref[...]) + brow_ref[...]
    lane = lax.broadcasted_iota(jnp.int32, g.shape, 1)
    g_ref[...] = jnp.where(lane >= MLSTM_HEADS, _log_sigmoid(g), g)
    gt = _dot_nt(wt_ref[...], h)[:2 * MLSTM_HEADS] + bcol_ref[...]
    row = lax.broadcasted_iota(jnp.int32, gt.shape, 0)
    gt_ref[...] = jnp.where(row >= MLSTM_HEADS, _log_sigmoid(gt), gt)


def _gates(h, w_pad, wt_pad, bias):
    t, d = h.shape
    ng = 2 * MLSTM_HEADS
    brow = jnp.zeros((1, GATE_PAD), F32).at[0, :ng].set(bias)
    bcol = bias.reshape(ng, 1)
    return pl.pallas_call(
        _gates_body,
        grid=(t // TM,),
        in_specs=[pl.BlockSpec((TM, d), lambda i: (i, 0)),
                  pl.BlockSpec((d, GATE_PAD), lambda i: (0, 0)),
                  pl.BlockSpec((2 * SUBLANES, d), lambda i: (0, 0)),
                  pl.BlockSpec((1, GATE_PAD), lambda i: (0, 0)),
                  pl.BlockSpec((ng, 1), lambda i: (0, 0))],
        out_specs=[pl.BlockSpec((TM, GATE_PAD), lambda i: (i, 0)),
                   pl.BlockSpec((ng, TM), lambda i: (0, i))],
        out_shape=[jax.ShapeDtypeStruct((t, GATE_PAD), F32),
                   jax.ShapeDtypeStruct((ng, t), F32)],
        compiler_params=_params(("parallel",)),
        name="gates",
    )(h, w_pad, wt_pad, brow, bcol)


def _mlstm_body(qk_ref, cw_ref, v_ref, og_ref, g_ref, gt_ref, hg_ref, out_ref, xbuf, c_scr, m_scr):
    nh, dkp, dv = MLSTM_HEADS, QK_PAD, MLSTM_V_DIM
    ell = CHUNK
    halo = SUBLANES

    @pl.when(pl.program_id(1) == 0)
    def _():
        xbuf[0:halo, :] = jnp.zeros((halo, xbuf.shape[1]), F32)
        c_scr[...] = jnp.zeros(c_scr.shape, F32)
        m_scr[...] = jnp.zeros(m_scr.shape, F32)

    xbuf[halo:halo + ell, :] = qk_ref[...]
    cw = cw_ref[...]
    y = cw[CONV_WIDTH - 1:CONV_WIDTH, :] * xbuf[halo:halo + ell, :]
    for j in range(CONV_WIDTH - 1):
        sh = CONV_WIDTH - 1 - j
        y = y + cw[j:j + 1, :] * xbuf[halo - sh:halo - sh + ell, :]
    xbuf[0:halo, :] = xbuf[ell:ell + halo, :]
    y = y * jax.nn.sigmoid(y)

    r_i = lax.broadcasted_iota(jnp.int32, (ell, ell), 0)
    c_i = lax.broadcasted_iota(jnp.int32, (ell, ell), 1)
    causal = c_i <= r_i
    tri = jnp.where(causal, 1.0, 0.0).astype(F32)
    tri_t = jnp.where(r_i <= c_i, 1.0, 0.0).astype(F32)
    gcol = g_ref[...]
    grow = gt_ref[...]
    bcol_all = jnp.dot(tri, gcol, preferred_element_type=F32, precision=lax.Precision.HIGHEST)
    brow_all = jnp.dot(grow, tri_t, preferred_element_type=F32, precision=lax.Precision.HIGHEST)

    lane = lax.broadcasted_iota(jnp.int32, (ell, LANES), 1)
    ones_col = jnp.where(lane == 0, 1.0, 0.0).astype(BF16)
    v_all = v_ref[...]
    og = og_ref[...]
    hg = hg_ref[...]

    for h in range(nh):
        q = (y[:, h * dkp:(h + 1) * dkp] * (MLSTM_QK_DIM ** -0.5)).astype(BF16)
        k_f = y[:, (nh + h) * dkp:(nh + h + 1) * dkp]
        k = k_f.astype(BF16)
        v_ext = jnp.concatenate([v_all[:, h * dv:(h + 1) * dv], ones_col], axis=-1)
        ig_col = gcol[:, h:h + 1]
        ig_row = grow[h:h + 1, :]
        b_col = bcol_all[:, nh + h:nh + h + 1]
        b_row = brow_all[nh + h:nh + h + 1, :]
        m_prev = m_scr[h][:, 0:1]
        c_prev = c_scr[h]

        log_w = jnp.where(causal, b_col - b_row + ig_row, -jnp.inf)
        log_inter = b_col + m_prev
        m_t = jnp.maximum(jnp.max(log_w, axis=-1, keepdims=True), log_inter)
        w = jnp.exp(log_w - m_t)
        w_inter = jnp.exp(log_inter - m_t)
        s = _dot_nt(q, k) * w
        inter = _dot(q, c_prev.astype(BF16))
        num = _dot(s.astype(BF16), v_ext[:, :dv]) + w_inter * inter[:, :dv]
        den = jnp.sum(s, axis=-1, keepdims=True) + w_inter * inter[:, dv:dv + 1]
        hh = num * (1.0 / jnp.maximum(jnp.abs(den), jnp.exp(-m_t)))

        g_last = b_col[ell - 1:ell, :]
        log_a = g_last - b_col + ig_col
        m_new = jnp.maximum(g_last + m_prev, jnp.max(log_a, axis=0, keepdims=True))
        decay = jnp.exp(g_last + m_prev - m_new)
        a = jnp.exp(log_a - m_new)
        c_scr[h] = decay * c_prev + _dot_tn((k_f * a).astype(BF16), v_ext)
        m_scr[h] = jnp.broadcast_to(m_new, (1, LANES))

        hn = _rms(hh, hg[:, h * dv:(h + 1) * dv])
        o_h = og[:, h * dv:(h + 1) * dv]
        out_ref[:, h * dv:(h + 1) * dv] = (jax.nn.sigmoid(o_h) * hn).astype(out_ref.dtype)


def _mlstm(qk, conv_w, v, og, g, gt, head_gain, bsz):
    t = qk.shape[0]
    nc = t // bsz // CHUNK
    nqk = qk.shape[1]
    return pl.pallas_call(
        _mlstm_body,
        grid=(bsz, nc),
        in_specs=[pl.BlockSpec((CHUNK, nqk), lambda b, c: (b * nc + c, 0)),
                  pl.BlockSpec((CONV_WIDTH, nqk), lambda b, c: (0, 0)),
                  pl.BlockSpec((CHUNK, MAIN_WIDTH), lambda b, c: (b * nc + c, 0)),
                  pl.BlockSpec((CHUNK, MAIN_WIDTH), lambda b, c: (b * nc + c, 0)),
                  pl.BlockSpec((CHUNK, GATE_PAD), lambda b, c: (b * nc + c, 0)),
                  pl.BlockSpec((2 * MLSTM_HEADS, CHUNK), lambda b, c: (0, b * nc + c)),
                  pl.BlockSpec((1, MAIN_WIDTH), lambda b, c: (0, 0))],
        out_specs=pl.BlockSpec((CHUNK, MAIN_WIDTH), lambda b, c: (b * nc + c, 0)),
        out_shape=jax.ShapeDtypeStruct((t, MAIN_WIDTH), BF16),
        scratch_shapes=[pltpu.VMEM((CHUNK + SUBLANES, nqk), F32),
                        pltpu.VMEM((MLSTM_HEADS, QK_PAD, MLSTM_V_DIM + LANES), F32),
                        pltpu.VMEM((MLSTM_HEADS, 1, LANES), F32)],
        compiler_params=_params(("parallel", "arbitrary")),
        name="mlstm",
    )(qk, conv_w, v, og, g, gt, head_gain.reshape(1, MAIN_WIDTH))


def _memkv_body(mem_ref, g_ref, w_ref, kg_ref, mk_ref, mv_ref):
    hn = _rms(mem_ref[...], g_ref[...]).astype(BF16)
    kv = _dot(hn, w_ref[...])
    kg = kg_ref[...]
    for h in range(MEM_HEADS):
        sl = slice(h * MEM_HEAD_DIM, (h + 1) * MEM_HEAD_DIM)
        mk_ref[:, sl] = _rms(kv[:, sl], kg).astype(mk_ref.dtype)
    mv_ref[...] = kv[:, MEM_WIDTH:].astype(mv_ref.dtype)


def _mem_kv(mem2d, mem_gain, w_mem_kv, mem_k_gain):
    rows = mem2d.shape[0]
    shp = jax.ShapeDtypeStruct((DEPTH, rows, MEM_WIDTH), BF16)
    return pl.pallas_call(
        _memkv_body,
        grid=(DEPTH,),
        in_specs=[pl.BlockSpec((rows, D_MODEL), lambda l: (0, 0)),
                  pl.BlockSpec((None, 1, D_MODEL), lambda l: (l, 0, 0)),
                  pl.BlockSpec((None, D_MODEL, 2 * MEM_WIDTH), lambda l: (l, 0, 0)),
                  pl.BlockSpec((None, 1, MEM_HEAD_DIM), lambda l: (l, 0, 0))],
        out_specs=[pl.BlockSpec((None, rows, MEM_WIDTH), lambda l: (l, 0, 0)),
                   pl.BlockSpec((None, rows, MEM_WIDTH), lambda l: (l, 0, 0))],
        out_shape=[shp, shp],
        compiler_params=_params(("parallel",)),
        name="mem_kv",
    )(mem2d, mem_gain.reshape(DEPTH, 1, D_MODEL), w_mem_kv, mem_k_gain.reshape(DEPTH, 1, MEM_HEAD_DIM))


def _mix_body(x_ref, main_ref, mq_ref, mk_ref, mv_ref, qg_ref, wmain_ref, wmem_ref, o_ref):
    mq = mq_ref[...]
    mk = mk_ref[...]
    mv = mv_ref[...]
    qg = qg_ref[...]
    heads = []
    for h in range(MEM_HEADS):
        sl = slice(h * MEM_HEAD_DIM, (h + 1) * MEM_HEAD_DIM)
        qn = _rms(mq[:, sl], qg).astype(BF16)
        s = _dot_nt(qn, mk[:, sl]) * (MEM_HEAD_DIM ** -0.5)
        e = jnp.exp(s - jnp.max(s, axis=-1, keepdims=True))
        oh = _dot(e.astype(BF16), mv[:, sl]) * (1.0 / jnp.sum(e, axis=-1, keepdims=True))
        heads.append(oh.astype(BF16))
    mem_out = jnp.concatenate(heads, axis=-1)
    o_ref[...] = x_ref[...] + _dot(main_ref[...], wmain_ref[...]) + _dot(mem_out, wmem_ref[...])


def _mixer_out(x, main, mq_arr, mq_block, mk, mv, q_gain, w_out, layer, seq):
    t, d = x.shape
    tiles_per_seq = seq // TM
    return pl.pallas_call(
        _mix_body,
        grid=(t // TM,),
        in_specs=[pl.BlockSpec((TM, d), lambda i: (i, 0)),
                  pl.BlockSpec((TM, MAIN_WIDTH), lambda i: (i, 0)),
                  pl.BlockSpec((TM, MEM_WIDTH), lambda i: (i, mq_block)),
                  pl.BlockSpec((None, N_MEM, MEM_WIDTH), lambda i: (layer, i // tiles_per_seq, 0)),
                  pl.BlockSpec((None, N_MEM, MEM_WIDTH), lambda i: (layer, i // tiles_per_seq, 0)),
                  pl.BlockSpec((None, 1, MEM_HEAD_DIM), lambda i: (layer, 0, 0)),
                  pl.BlockSpec((None, MAIN_WIDTH, d), lambda i: (layer, 0, 0)),
                  pl.BlockSpec((None, MEM_WIDTH, d), lambda i: (layer, MAIN_WIDTH // MEM_WIDTH, 0))],
        out_specs=pl.BlockSpec((TM, d), lambda i: (i, 0)),
        out_shape=jax.ShapeDtypeStruct((t, d), F32),
        compiler_params=_params(("parallel",)),
        name="mixer_out",
    )(x, main, mq_arr, mk, mv, q_gain, w_out, w_out)


def _rope_body(pos_ref, freq_ref, c_ref, s_ref):
    ang = pos_ref[...].astype(F32) * freq_ref[...]
    lane = lax.broadcasted_iota(jnp.int32, ang.shape, 1)
    half = MLA_ROPE_DIM // 2
    cos = jnp.cos(ang)
    sin = jnp.sin(ang)
    c_ref[...] = jnp.where(lane < MLA_ROPE_DIM, cos, 0.0)
    s_ref[...] = jnp.where(lane < half, -sin, jnp.where(lane < MLA_ROPE_DIM, sin, 0.0))


def _rope_tables(positions):
    t = positions.size
    tr = min(TROPE, t)
    half = MLA_ROPE_DIM // 2
    inv_freq = ROPE_THETA ** (-jnp.arange(0, MLA_ROPE_DIM, 2, dtype=F32) / MLA_ROPE_DIM)
    freq = jnp.tile(inv_freq, LANES // half).reshape(1, LANES)
    shp = jax.ShapeDtypeStruct((t, LANES), F32)
    return pl.pallas_call(
        _rope_body,
        grid=(t // tr,),
        in_specs=[pl.BlockSpec((tr, 1), lambda i: (i, 0)),
                  pl.BlockSpec((1, LANES), lambda i: (0, 0))],
        out_specs=[pl.BlockSpec((tr, LANES), lambda i: (i, 0)),
                   pl.BlockSpec((tr, LANES), lambda i: (i, 0))],
        out_shape=[shp, shp],
        compiler_params=_params(("parallel",)),
        name="rope_tables",
    )(positions.reshape(t, 1), freq)


def _rope(u, cmul, smul):
    return u * cmul + (pltpu.roll(u, 96, 1) + pltpu.roll(u, 32, 1)) * smul


def _kv_body(h_ref, wd_ref, lg_ref, wu_ref, kgn_ref, kgr_ref, c_ref, s_ref, k_ref, v_ref):
    low = _dot(h_ref[...], wd_ref[...])
    ckv = _rms(low[:, :KV_LORA_RANK], lg_ref[...]).astype(BF16)
    kv = _dot(ckv, wu_ref[...])
    pe = low[:, KV_LORA_RANK:]
    pe_ss = jnp.sum(pe * pe, axis=-1, keepdims=True)
    pe_rot = _rope(pe * kgr_ref[...], c_ref[...], s_ref[...])
    kgn = kgn_ref[...]
    for h in range(MLA_HEADS):
        kn = kv[:, h * HEAD_PAD:h * HEAD_PAD + MLA_NOPE_DIM]
        ms = (jnp.sum(kn * kn, axis=-1, keepdims=True) + pe_ss) * (1.0 / MLA_QK_DIM)
        rinv = lax.rsqrt(ms + EPS)
        k_ref[0, h, :, 0:MLA_NOPE_DIM] = ((kn * rinv) * kgn).astype(k_ref.dtype)
        k_ref[0, h, :, MLA_NOPE_DIM:HEAD_PAD] = (pe_rot * rinv).astype(k_ref.dtype)
        v_ref[0, h] = kv[:, h * HEAD_PAD + MLA_NOPE_DIM:(h + 1) * HEAD_PAD].astype(v_ref.dtype)


def _shared_kv(h, w_dkv_pad, latent_gain, w_ukv, kg_nope, kg_rope_pad, cmul, smul, bsz):
    t, d = h.shape
    seq = t // bsz
    tps = seq // TM
    return pl.pallas_call(
        _kv_body,
        grid=(t // TM,),
        in_specs=[pl.BlockSpec((TM, d), lambda i: (i, 0)),
                  pl.BlockSpec((d, KV_DOWN_PAD), lambda i: (0, 0)),
                  pl.BlockSpec((1, KV_LORA_RANK), lambda i: (0, 0)),
                  pl.BlockSpec((KV_LORA_RANK, MLA_HEADS * HEAD_PAD), lambda i: (0, 0)),
                  pl.BlockSpec((1, MLA_NOPE_DIM), lambda i: (0, 0)),
                  pl.BlockSpec((1, LANES), lambda i: (0, 0)),
                  pl.BlockSpec((TM, LANES), lambda i: (i, 0)),
                  pl.BlockSpec((TM, LANES), lambda i: (i, 0))],
        out_specs=[pl.BlockSpec((1, MLA_HEADS, TM, HEAD_PAD), lambda i: (i // tps, 0, i % tps, 0)),
                   pl.BlockSpec((1, MLA_HEADS, TM, MLA_V_DIM), lambda i: (i // tps, 0, i % tps, 0))],
        out_shape=[jax.ShapeDtypeStruct((bsz, MLA_HEADS, seq, HEAD_PAD), BF16),
                   jax.ShapeDtypeStruct((bsz, MLA_HEADS, seq, MLA_V_DIM), BF16)],
        compiler_params=_params(("parallel",)),
        name="shared_kv",
    )(h, w_dkv_pad, latent_gain, w_ukv, kg_nope, kg_rope_pad, cmul, smul)


def _q_body(cq_ref, lg_ref, wu_ref, qg_ref, c_ref, s_ref, q_ref):
    cq = _rms(cq_ref[...], lg_ref[...], n=Q_LORA_RANK).astype(BF16)
    q = _dot(cq, wu_ref[...])
    qg = qg_ref[...]
    cmul = c_ref[...]
    smul = s_ref[...]
    for h in range(MLA_HEADS):
        qh = _rms(q[:, h * HEAD_PAD:(h + 1) * HEAD_PAD], qg, n=MLA_QK_DIM)
        q_ref[0, h, :, 0:MLA_NOPE_DIM] = qh[:, :MLA_NOPE_DIM].astype(q_ref.dtype)
        q_ref[0, h, :, MLA_NOPE_DIM:HEAD_PAD] = _rope(qh[:, MLA_NOPE_DIM:], cmul, smul).astype(q_ref.dtype)


def _mla_q(cqmq, latent_gain_pad, w_uq_pad, q_gain_pad, cmul, smul, bsz):
    t = cqmq.shape[0]
    seq = t // bsz
    tps = seq // TM
    return pl.pallas_call(
        _q_body,
        grid=(t // TM,),
        in_specs=[pl.BlockSpec((TM, Q_LORA_PAD), lambda i: (i, 0)),
                  pl.BlockSpec((1, Q_LORA_PAD), lambda i: (0, 0)),
                  pl.BlockSpec((Q_LORA_PAD, MLA_HEADS * HEAD_PAD), lambda i: (0, 0)),
                  pl.BlockSpec((1, HEAD_PAD), lambda i: (0, 0)),
                  pl.BlockSpec((TM, LANES), lambda i: (i, 0)),
                  pl.BlockSpec((TM, LANES), lambda i: (i, 0))],
        out_specs=pl.BlockSpec((1, MLA_HEADS, TM, HEAD_PAD), lambda i: (i // tps, 0, i % tps, 0)),
        out_shape=jax.ShapeDtypeStruct((bsz, MLA_HEADS, seq, HEAD_PAD), BF16),
        compiler_params=_params(("parallel",)),
        name="mla_q",
    )(cqmq, latent_gain_pad, w_uq_pad, q_gain_pad, cmul, smul)


def _flash_body(qi_tab, ki_tab, last_tab, q_ref, k_ref, v_ref, o_ref, m_scr, l_scr, acc_scr):
    p_id = pl.program_id(1)
    diag = qi_tab[p_id] == ki_tab[p_id]
    scale = MLA_QK_DIM ** -0.5

    def scores(h):
        return _dot_nt(q_ref[0, h], k_ref[0, h]) * scale

    @pl.when(diag)
    def _():
        r_i = lax.broadcasted_iota(jnp.int32, (TQ, TQ), 0)
        c_i = lax.broadcasted_iota(jnp.int32, (TQ, TQ), 1)
        keep = c_i <= r_i
        for h in range(MLA_HEADS):
            s = jnp.where(keep, scores(h), -jnp.inf)
            m = jnp.max(s, axis=-1, keepdims=True)
            e = jnp.exp(s - m)
            m_scr[h] = m
            l_scr[h] = jnp.sum(e, axis=-1, keepdims=True)
            acc_scr[h] = _dot(e.astype(BF16), v_ref[0, h])

    @pl.when(jnp.logical_not(diag))
    def _():
        for h in range(MLA_HEADS):
            s = scores(h)
            m_prev = m_scr[h]
            m = jnp.maximum(m_prev, jnp.max(s, axis=-1, keepdims=True))
            alpha = jnp.exp(m_prev - m)
            e = jnp.exp(s - m)
            m_scr[h] = m
            l_scr[h] = alpha * l_scr[h] + jnp.sum(e, axis=-1, keepdims=True)
            acc_scr[h] = alpha * acc_scr[h] + _dot(e.astype(BF16), v_ref[0, h])

    @pl.when(last_tab[p_id] == 1)
    def _():
        for h in range(MLA_HEADS):
            o_ref[:, h * MLA_V_DIM:(h + 1) * MLA_V_DIM] = (
                acc_scr[h] * (1.0 / l_scr[h])).astype(o_ref.dtype)


def _flash(q, k, v):
    bsz, nh, seq, _ = q.shape
    nq = seq // TQ
    qi, ki, last = [], [], []
    for a in range(nq):
        order = [a] + list(range(a))
        for n, b in enumerate(order):
            qi.append(a)
            ki.append(b)
            last.append(1 if n == len(order) - 1 else 0)
    npairs = len(qi)
    tabs = [jnp.asarray(x, jnp.int32) for x in (qi, ki, last)]
    grid_spec = pltpu.PrefetchScalarGridSpec(
        num_scalar_prefetch=3,
        grid=(bsz, npairs),
        in_specs=[pl.BlockSpec((1, nh, TQ, HEAD_PAD), lambda b, p, qt, kt, lt: (b, 0, qt[p], 0)),
                  pl.BlockSpec((1, nh, TQ, HEAD_PAD), lambda b, p, qt, kt, lt: (b, 0, kt[p], 0)),
                  pl.BlockSpec((1, nh, TQ, MLA_V_DIM), lambda b, p, qt, kt, lt: (b, 0, kt[p], 0))],
        out_specs=pl.BlockSpec((TQ, nh * MLA_V_DIM), lambda b, p, qt, kt, lt: (b * nq + qt[p], 0)),
        scratch_shapes=[pltpu.VMEM((nh, TQ, 1), F32),
                        pltpu.VMEM((nh, TQ, 1), F32),
                        pltpu.VMEM((nh, TQ, MLA_V_DIM), F32)],
    )
    return pl.pallas_call(
        _flash_body,
        grid_spec=grid_spec,
        out_shape=jax.ShapeDtypeStruct((bsz * seq, nh * MLA_V_DIM), BF16),
        compiler_params=_params(("parallel", "arbitrary")),
        name="mla_attention",
    )(*tabs, q, k, v)


def _pad_heads(w, nh, dh, dh_pad):
    lead = w.shape[:-1]
    w = w.reshape(lead + (nh, dh))
    w = jnp.pad(w, [(0, 0)] * len(lead) + [(0, 0), (0, dh_pad - dh)])
    return w.reshape(lead + (nh * dh_pad,))


def _pad_last(w, n):
    return jnp.pad(w, [(0, 0)] * (w.ndim - 1) + [(0, n - w.shape[-1])])


def kernel(x, mem, positions, ffn1_gain, ffn1_w_in, ffn1_w_out, mix_gain, w_out, mem_gain, w_mem_kv, mem_q_gain, mem_k_gain, a_w_in, a_b_gates, a_conv, a_head_gain, kv_gain, w_dkv, kv_latent_gain, w_ukv, k_gain, b_w_in, b_q_latent_gain, b_w_uq, b_q_gain, ffn2_gain, ffn2_w_in, ffn2_w_out):
    bsz, seq, d = x.shape
    t = bsz * seq
    assert d == D_MODEL and seq % TM == 0 and seq % CHUNK == 0 and seq % TQ == 0 and t % min(TROPE, t) == 0

    ffn1_wi, ffn1_wo = ffn1_w_in.astype(BF16), ffn1_w_out.astype(BF16)
    ffn2_wi, ffn2_wo = ffn2_w_in.astype(BF16), ffn2_w_out.astype(BF16)
    w_out_b = w_out.astype(BF16)

    nqk = MLSTM_HEADS * MLSTM_QK_DIM
    c0, c1, c2, c3 = 2 * nqk, 2 * nqk + MAIN_WIDTH, 2 * nqk + 2 * MAIN_WIDTH, 2 * nqk + 2 * MAIN_WIDTH + 2 * MLSTM_HEADS

    cmul, smul = _rope_tables(positions)
    mk, mv = _mem_kv(mem.reshape(bsz * N_MEM, d), mem_gain, w_mem_kv.astype(BF16), mem_k_gain)

    xs = x.reshape(t, d)
    k_sh = v_sh = None
    for layer in range(DEPTH):
        xs, hmix = _ffn(xs, ffn1_gain[layer], ffn1_wi, ffn1_wo, layer, next_gain=mix_gain[layer])
        if layer < N_A_LAYERS:
            w_in = a_w_in[layer]
            w_qk = jnp.concatenate([_pad_heads(w_in[:, :nqk], MLSTM_HEADS, MLSTM_QK_DIM, QK_PAD),
                                    _pad_heads(w_in[:, nqk:c0], MLSTM_HEADS, MLSTM_QK_DIM, QK_PAD)], axis=-1)
            conv_w = jnp.concatenate([_pad_heads(a_conv[layer][:, :nqk], MLSTM_HEADS, MLSTM_QK_DIM, QK_PAD),
                                      _pad_heads(a_conv[layer][:, nqk:], MLSTM_HEADS, MLSTM_QK_DIM, QK_PAD)], axis=-1)
            w_g = w_in[:, c2:c3]
            qk = _matmul(hmix, w_qk.astype(BF16), F32, 1024)
            v = _matmul(hmix, w_in[:, c0:c1].astype(BF16), BF16, MAIN_WIDTH // 2)
            og = _matmul(hmix, w_in[:, c1:c2].astype(BF16), F32, MAIN_WIDTH // 2)
            mq_arr = _matmul(hmix, w_in[:, c3:].astype(BF16), F32, MEM_WIDTH)
            g, gt = _gates(hmix, _pad_last(w_g, GATE_PAD).astype(BF16),
                           jnp.pad(w_g.T, ((0, 2 * SUBLANES - 2 * MLSTM_HEADS), (0, 0))).astype(BF16),
                           a_b_gates[layer])
            main = _mlstm(qk, conv_w, v, og, g, gt, a_head_gain[layer], bsz)
            mq_block = 0
        else:
            j = layer - N_A_LAYERS
            w_in = b_w_in[j]
            w_b = jnp.concatenate([_pad_last(w_in[:, :Q_LORA_RANK], Q_LORA_PAD), w_in[:, Q_LORA_RANK:]], axis=-1)
            mq_arr = _matmul(hmix, w_b.astype(BF16), F32, Q_LORA_PAD)
            w_uq = jnp.pad(_pad_heads(b_w_uq[j], MLA_HEADS, MLA_QK_DIM, HEAD_PAD),
                           ((0, Q_LORA_PAD - Q_LORA_RANK), (0, 0))).astype(BF16)
            q = _mla_q(mq_arr, _pad_last(b_q_latent_gain[j], Q_LORA_PAD).reshape(1, Q_LORA_PAD), w_uq,
                       _pad_last(b_q_gain[j], HEAD_PAD).reshape(1, HEAD_PAD), cmul, smul, bsz)
            main = _flash(q, k_sh, v_sh)
            mq_block = 1
        xs = _mixer_out(xs, main, mq_arr, mq_block, mk, mv, mem_q_gain.reshape(DEPTH, 1, MEM_HEAD_DIM),
                        w_out_b, layer, seq)
        if layer == N_A_LAYERS - 1:
            xs, hkv = _ffn(xs, ffn2_gain[layer], ffn2_wi, ffn2_wo, layer, next_gain=kv_gain)
            k_sh, v_sh = _shared_kv(hkv, _pad_last(w_dkv, KV_DOWN_PAD).astype(BF16),
                                    kv_latent_gain.reshape(1, KV_LORA_RANK), w_ukv.astype(BF16),
                                    k_gain[:MLA_NOPE_DIM].reshape(1, MLA_NOPE_DIM),
                                    _pad_last(k_gain[MLA_NOPE_DIM:], LANES).reshape(1, LANES), cmul, smul, bsz)
        else:
            xs = _ffn(xs, ffn2_gain[layer], ffn2_wi, ffn2_wo, layer)
    return xs.reshape(bsz, seq, d)
```

```python
import functools

import jax
import jax.numpy as jnp
from jax import lax
from jax.experimental import pallas as pl
from jax.experimental.pallas import tpu as pltpu

F32 = jnp.float32
BF16 = jnp.bfloat16

D_MODEL = 2048
DEPTH = 4
N_MEM = 256
N_A_LAYERS = DEPTH // 2
EPS = 1e-6
MEM_HEADS = 4
MEM_WIDTH = D_MODEL // 4
MEM_HEAD_DIM = MEM_WIDTH // MEM_HEADS
MAIN_WIDTH = D_MODEL - MEM_WIDTH
MLSTM_HEADS = 4
MLSTM_V_DIM = MAIN_WIDTH // MLSTM_HEADS
MLSTM_QK_DIM = MLSTM_V_DIM // 2
CONV_WIDTH = 4
MLA_NOPE_DIM = 128
MLA_ROPE_DIM = 64
MLA_QK_DIM = MLA_NOPE_DIM + MLA_ROPE_DIM
MLA_V_DIM = 128
MLA_HEADS = MAIN_WIDTH // MLA_V_DIM
Q_LORA_RANK = 448
KV_LORA_RANK = 512
ROPE_THETA = 10000.0
D_FF = 5632
LOG2_E = 1.4426950408889634

LANES = 128
SUBLANES = 8
V7X_VMEM_BYTES = 64 * 1024 * 1024

QK_PAD = 256
Q_LORA_PAD = 512
HEAD_PAD = 256
GATE_PAD = LANES
KV_DOWN_PAD = KV_LORA_RANK + LANES
V_EXT = MLA_V_DIM + LANES

TM = 512
TF = 512
CHUNK = 256
TQ = 512
TROPE = 2048
VMEM_LIMIT = 56 * 1024 * 1024


def _params(sem):
    return pltpu.CompilerParams(dimension_semantics=sem, vmem_limit_bytes=VMEM_LIMIT)


def _rms(x, gain, n=None):
    if n is None:
        ms = jnp.mean(x * x, axis=-1, keepdims=True)
    else:
        ms = jnp.sum(x * x, axis=-1, keepdims=True) * (1.0 / n)
    return (x * lax.rsqrt(ms + EPS)) * gain


def _log_sigmoid(x):
    return -(jnp.maximum(-x, 0.0) + jnp.log1p(jnp.exp(-jnp.abs(x))))


def _dot(a, b):
    return jnp.dot(a, b, preferred_element_type=F32)


def _dot_nt(a, b):
    return lax.dot_general(a, b, (((1,), (1,)), ((), ())), preferred_element_type=F32)


def _dot_tn(a, b):
    return lax.dot_general(a, b, (((0,), (0,)), ((), ())), preferred_element_type=F32)


def _ffn_body(emit_next, x_ref, g_ref, wg_ref, wu_ref, wo_ref, *rest):
    if emit_next:
        ng_ref, o_ref, hn_ref, h_scr = rest
    else:
        o_ref, h_scr = rest
    j = pl.program_id(1)

    @pl.when(j == 0)
    def _():
        x = x_ref[...]
        h_scr[...] = _rms(x, g_ref[...]).astype(BF16)
        o_ref[...] = x

    h = h_scr[...]
    g = _dot(h, wg_ref[...])
    u = _dot(h, wu_ref[...])
    a = ((g * jax.nn.sigmoid(g)) * u) * 0.5
    o_ref[...] += _dot(a.astype(BF16), wo_ref[...])

    if emit_next:
        @pl.when(j == pl.num_programs(1) - 1)
        def _():
            hn_ref[...] = _rms(o_ref[...], ng_ref[...]).astype(BF16)


def _ffn(x, gain, w_in, w_out, layer, next_gain=None):
    t, d = x.shape
    nf = D_FF // TF
    emit = next_gain is not None
    in_specs = [
        pl.BlockSpec((TM, d), lambda i, j: (i, 0)),
        pl.BlockSpec((1, d), lambda i, j: (0, 0)),
        pl.BlockSpec((None, d, TF), lambda i, j: (layer, 0, j)),
        pl.BlockSpec((None, d, TF), lambda i, j: (layer, 0, j + nf)),
        pl.BlockSpec((None, TF, d), lambda i, j: (layer, j, 0)),
    ]
    args = [x, gain.reshape(1, d), w_in, w_in, w_out]
    out_shape = [jax.ShapeDtypeStruct((t, d), F32)]
    out_specs = [pl.BlockSpec((TM, d), lambda i, j: (i, 0))]
    if emit:
        in_specs.append(pl.BlockSpec((1, d), lambda i, j: (0, 0)))
        args.append(next_gain.reshape(1, d))
        out_shape.append(jax.ShapeDtypeStruct((t, d), BF16))
        out_specs.append(pl.BlockSpec((TM, d), lambda i, j: (i, 0)))
    res = pl.pallas_call(
        functools.partial(_ffn_body, emit),
        grid=(t // TM, nf),
        in_specs=in_specs,
        out_specs=out_specs,
        out_shape=out_shape,
        scratch_shapes=[pltpu.VMEM((TM, d), BF16)],
        compiler_params=_params(("parallel", "arbitrary")),
        name="ffn_next" if emit else "ffn",
    )(*args)
    return (res[0], res[1]) if emit else res[0]


def _mm_body(a_ref, w_ref, o_ref):
    o_ref[...] = _dot(a_ref[...], w_ref[...]).astype(o_ref.dtype)


def _matmul(a, w, out_dtype, tn):
    t, k = a.shape
    n = w.shape[1]
    return pl.pallas_call(
        _mm_body,
        grid=(t // TM, n // tn),
        in_specs=[pl.BlockSpec((TM, k), lambda i, j: (i, 0)),
                  pl.BlockSpec((k, tn), lambda i, j: (0, j))],
        out_specs=pl.BlockSpec((TM, tn), lambda i, j: (i, j)),
        out_shape=jax.ShapeDtypeStruct((t, n), out_dtype),
        compiler_params=_params(("parallel", "arbitrary")),
        name="proj",
    )(a, w)


def _gates_body(h_ref, w_ref, wt_ref, brow_ref, bcol_ref, g_ref, gt_ref):
    h = h_ref[...]
    g = _dot(h, w_ref[...]) + brow_ref[...]
    lane = lax.broadcasted_iota(jnp.int32, g.shape, 1)
    g_ref[...] = jnp.where(lane >= MLSTM_HEADS, _log_sigmoid(g), g)
    gt = _dot_nt(wt_ref[...], h)[:2 * MLSTM_HEADS] + bcol_ref[...]
    row = lax.broadcasted_iota(jnp.int32, gt.shape, 0)
    gt_ref[...] = jnp.where(row >= MLSTM_HEADS, _log_sigmoid(gt), gt)


def _gates(h, w_pad, wt_pad, bias):
    t, d = h.shape
    ng = 2 * MLSTM_HEADS
    brow = jnp.zeros((1, GATE_PAD), F32).at[0, :ng].set(bias)
    bcol = bias.reshape(ng, 1)
    return pl.pallas_call(
        _gates_body,
        grid=(t // TM,),
        in_specs=[pl.BlockSpec((TM, d), lambda i: (i, 0)),
                  pl.BlockSpec((d, GATE_PAD), lambda i: (0, 0)),
                  pl.BlockSpec((2 * SUBLANES, d), lambda i: (0, 0)),
                  pl.BlockSpec((1, GATE_PAD), lambda i: (0, 0)),
                  pl.BlockSpec((ng, 1), lambda i: (0, 0))],
        out_specs=[pl.BlockSpec((TM, GATE_PAD), lambda i: (i, 0)),
                   pl.BlockSpec((ng, TM), lambda i: (0, i))],
        out_shape=[jax.ShapeDtypeStruct((t, GATE_PAD), F32),
                   jax.ShapeDtypeStruct((ng, t), F32)],
        compiler_params=_params(("parallel",)),
        name="gates",
    )(h, w_pad, wt_pad, brow, bcol)


def _mlstm_body(qk_ref, cw_ref, v_ref, og_ref, g_ref, gt_ref, hg_ref, out_ref, xbuf, c_scr, m_scr):
    nh, dkp, dv = MLSTM_HEADS, QK_PAD, MLSTM_V_DIM
    ell = CHUNK
    halo = SUBLANES

    @pl.when(pl.program_id(1) == 0)
    def _():
        xbuf[0:halo, :] = jnp.zeros((halo, xbuf.shape[1]), F32)
        c_scr[...] = jnp.zeros(c_scr.shape, F32)
        m_scr[...] = jnp.zeros(m_scr.shape, F32)

    xbuf[halo:halo + ell, :] = qk_ref[...]
    cw = cw_ref[...]
    y = cw[CONV_WIDTH - 1:CONV_WIDTH, :] * xbuf[halo:halo + ell, :]
    for j in range(CONV_WIDTH - 1):
        sh = CONV_WIDTH - 1 - j
        y = y + cw[j:j + 1, :] * xbuf[halo - sh:halo - sh + ell, :]
    xbuf[0:halo, :] = xbuf[ell:ell + halo, :]
    y = y * jax.nn.sigmoid(y)

    r_i = lax.broadcasted_iota(jnp.int32, (ell, ell), 0)
    c_i = lax.broadcasted_iota(jnp.int32, (ell, ell), 1)
    causal = c_i <= r_i
    tri = jnp.where(causal, 1.0, 0.0).astype(F32)
    tri_t = jnp.where(r_i <= c_i, 1.0, 0.0).astype(F32)
    gcol = g_ref[...]
    grow = gt_ref[...]
    bcol_all = jnp.dot(tri, gcol, preferred_element_type=F32, precision=lax.Precision.HIGHEST)
    brow_all = jnp.dot(grow, tri_t, preferred_element_type=F32, precision=lax.Precision.HIGHEST)

    lane = lax.broadcasted_iota(jnp.int32, (ell, LANES), 1)
    ones_col = jnp.where(lane == 0, 1.0, 0.0).astype(BF16)
    v_all = v_ref[...]
    og = og_ref[...]
    hg = hg_ref[...]

    for h in range(nh):
        q = (y[:, h * dkp:(h + 1) * dkp] * (MLSTM_QK_DIM ** -0.5)).astype(BF16)
        k_f = y[:, (nh + h) * dkp:(nh + h + 1) * dkp]
        k = k_f.astype(BF16)
        v_ext = jnp.concatenate([v_all[:, h * dv:(h + 1) * dv], ones_col], axis=-1)
        ig_col = gcol[:, h:h + 1]
        ig_row = grow[h:h + 1, :]
        b_col = bcol_all[:, nh + h:nh + h + 1]
        b_row = brow_all[nh + h:nh + h + 1, :]
        m_prev = m_scr[h][:, 0:1]
        c_prev = c_scr[h]

        log_w = jnp.where(causal, b_col - b_row + ig_row, -jnp.inf)
        log_inter = b_col + m_prev
        m_t = jnp.maximum(jnp.max(log_w, axis=-1, keepdims=True), log_inter)
        w = jnp.exp(log_w - m_t)
        w_inter = jnp.exp(log_inter - m_t)
        s = _dot_nt(q, k) * w
        inter = _dot(q, c_prev.astype(BF16))
        num = _dot(s.astype(BF16), v_ext[:, :dv]) + w_inter * inter[:, :dv]
        den = jnp.sum(s, axis=-1, keepdims=True) + w_inter * inter[:, dv:dv + 1]
        hh = num * (1.0 / jnp.maximum(jnp.abs(den), jnp.exp(-m_t)))

        g_last = b_col[ell - 1:ell, :]
        log_a = g_last - b_col + ig_col
        m_new = jnp.maximum(g_last + m_prev, jnp.max(log_a, axis=0, keepdims=True))
        decay = jnp.exp(g_last + m_prev - m_new)
        a = jnp.exp(log_a - m_new)
        c_scr[h] = decay * c_prev + _dot_tn((k_f * a).astype(BF16), v_ext)
        m_scr[h] = jnp.broadcast_to(m_new, (1, LANES))

        hn = _rms(hh, hg[:, h * dv:(h + 1) * dv])
        o_h = og[:, h * dv:(h + 1) * dv]
        out_ref[:, h * dv:(h + 1) * dv] = (jax.nn.sigmoid(o_h) * hn).astype(out_ref.dtype)


def _mlstm(qk, conv_w, v, og, g, gt, head_gain, bsz):
    t = qk.shape[0]
    nc = t // bsz // CHUNK
    nqk = qk.shape[1]
    return pl.pallas_call(
        _mlstm_body,
        grid=(bsz, nc),
        in_specs=[pl.BlockSpec((CHUNK, nqk), lambda b, c: (b * nc + c, 0)),
                  pl.BlockSpec((CONV_WIDTH, nqk), lambda b, c: (0, 0)),
                  pl.BlockSpec((CHUNK, MAIN_WIDTH), lambda b, c: (b * nc + c, 0)),
                  pl.BlockSpec((CHUNK, MAIN_WIDTH), lambda b, c: (b * nc + c, 0)),
                  pl.BlockSpec((CHUNK, GATE_PAD), lambda b, c: (b * nc + c, 0)),
                  pl.BlockSpec((2 * MLSTM_HEADS, CHUNK), lambda b, c: (0, b * nc + c)),
                  pl.BlockSpec((1, MAIN_WIDTH), lambda b, c: (0, 0))],
        out_specs=pl.BlockSpec((CHUNK, MAIN_WIDTH), lambda b, c: (b * nc + c, 0)),
        out_shape=jax.ShapeDtypeStruct((t, MAIN_WIDTH), BF16),
        scratch_shapes=[pltpu.VMEM((CHUNK + SUBLANES, nqk), F32),
                        pltpu.VMEM((MLSTM_HEADS, QK_PAD, MLSTM_V_DIM + LANES), F32),
                        pltpu.VMEM((MLSTM_HEADS, 1, LANES), F32)],
        compiler_params=_params(("parallel", "arbitrary")),
        name="mlstm",
    )(qk, conv_w, v, og, g, gt, head_gain.reshape(1, MAIN_WIDTH))


def _memkv_body(mem_ref, g_ref, w_ref, kg_ref, mk_ref, mv_ref):
    hn = _rms(mem_ref[...], g_ref[...]).astype(BF16)
    kv = _dot(hn, w_ref[...])
    kg = kg_ref[...]
    for h in range(MEM_HEADS):
        sl = slice(h * MEM_HEAD_DIM, (h + 1) * MEM_HEAD_DIM)
        mk_ref[:, sl] = _rms(kv[:, sl], kg).astype(mk_ref.dtype)
    mv_ref[...] = kv[:, MEM_WIDTH:].astype(mv_ref.dtype)


def _mem_kv(mem2d, mem_gain, w_mem_kv, mem_k_gain):
    rows = mem2d.shape[0]
    shp = jax.ShapeDtypeStruct((DEPTH, rows, MEM_WIDTH), BF16)
    return pl.pallas_call(
        _memkv_body,
        grid=(DEPTH,),
        in_specs=[pl.BlockSpec((rows, D_MODEL), lambda l: (0, 0)),
                  pl.BlockSpec((None, 1, D_MODEL), lambda l: (l, 0, 0)),
                  pl.BlockSpec((None, D_MODEL, 2 * MEM_WIDTH), lambda l: (l, 0, 0)),
                  pl.BlockSpec((None, 1, MEM_HEAD_DIM), lambda l: (l, 0, 0))],
        out_specs=[pl.BlockSpec((None, rows, MEM_WIDTH), lambda l: (l, 0, 0)),
                   pl.BlockSpec((None, rows, MEM_WIDTH), lambda l: (l, 0, 0))],
        out_shape=[shp, shp],
        compiler_params=_params(("parallel",)),
        name="mem_kv",
    )(mem2d, mem_gain.reshape(DEPTH, 1, D_MODEL), w_mem_kv, mem_k_gain.reshape(DEPTH, 1, MEM_HEAD_DIM))


def _mix_body(x_ref, main_ref, mq_ref, mk_ref, mv_ref, qg_ref, wmain_ref, wmem_ref, o_ref):
    mq = mq_ref[...]
    mk = mk_ref[...]
    mv = mv_ref[...]
    qg = qg_ref[...]
    heads = []
    for h in range(MEM_HEADS):
        sl = slice(h * MEM_HEAD_DIM, (h + 1) * MEM_HEAD_DIM)
        qn = _rms(mq[:, sl], qg).astype(BF16)
        s = _dot_nt(qn, mk[:, sl]) * (MEM_HEAD_DIM ** -0.5)
        e = jnp.exp(s - jnp.max(s, axis=-1, keepdims=True))
        oh = _dot(e.astype(BF16), mv[:, sl]) * (1.0 / jnp.sum(e, axis=-1, keepdims=True))
        heads.append(oh.astype(BF16))
    mem_out = jnp.concatenate(heads, axis=-1)
    o_ref[...] = x_ref[...] + _dot(main_ref[...], wmain_ref[...]) + _dot(mem_out, wmem_ref[...])


def _mixer_out(x, main, mq_arr, mq_block, mk, mv, q_gain, w_out, layer, seq):
    t, d = x.shape
    tiles_per_seq = seq // TM
    return pl.pallas_call(
        _mix_body,
        grid=(t // TM,),
        in_specs=[pl.BlockSpec((TM, d), lambda i: (i, 0)),
                  pl.BlockSpec((TM, MAIN_WIDTH), lambda i: (i, 0)),
                  pl.BlockSpec((TM, MEM_WIDTH), lambda i: (i, mq_block)),
                  pl.BlockSpec((None, N_MEM, MEM_WIDTH), lambda i: (layer, i // tiles_per_seq, 0)),
                  pl.BlockSpec((None, N_MEM, MEM_WIDTH), lambda i: (layer, i // tiles_per_seq, 0)),
                  pl.BlockSpec((None, 1, MEM_HEAD_DIM), lambda i: (layer, 0, 0)),
                  pl.BlockSpec((None, MAIN_WIDTH, d), lambda i: (layer, 0, 0)),
                  pl.BlockSpec((None, MEM_WIDTH, d), lambda i: (layer, MAIN_WIDTH // MEM_WIDTH, 0))],
        out_specs=pl.BlockSpec((TM, d), lambda i: (i, 0)),
        out_shape=jax.ShapeDtypeStruct((t, d), F32),
        compiler_params=_params(("parallel",)),
        name="mixer_out",
    )(x, main, mq_arr, mk, mv, q_gain, w_out, w_out)


def _rope_body(pos_ref, freq_ref, c_ref, s_ref):
    ang = pos_ref[...].astype(F32) * freq_ref[...]
    lane = lax.broadcasted_iota(jnp.int32, ang.shape, 1)
    half = MLA_ROPE_DIM // 2
    cos = jnp.cos(ang)
    sin = jnp.sin(ang)
    c_ref[...] = jnp.where(lane < MLA_ROPE_DIM, cos, 0.0)
    s_ref[...] = jnp.where(lane < half, -sin, jnp.where(lane < MLA_ROPE_DIM, sin, 0.0))


def _rope_tables(positions):
    t = positions.size
    tr = min(TROPE, t)
    half = MLA_ROPE_DIM // 2
    inv_freq = ROPE_THETA ** (-jnp.arange(0, MLA_ROPE_DIM, 2, dtype=F32) / MLA_ROPE_DIM)
    freq = jnp.tile(inv_freq, LANES // half).reshape(1, LANES)
    shp = jax.ShapeDtypeStruct((t, LANES), F32)
    return pl.pallas_call(
        _rope_body,
        grid=(t // tr,),
        in_specs=[pl.BlockSpec((tr, 1), lambda i: (i, 0)),
                  pl.BlockSpec((1, LANES), lambda i: (0, 0))],
        out_specs=[pl.BlockSpec((tr, LANES), lambda i: (i, 0)),
                   pl.BlockSpec((tr, LANES), lambda i: (i, 0))],
        out_shape=[shp, shp],
        compiler_params=_params(("parallel",)),
        name="rope_tables",
    )(positions.reshape(t, 1), freq)


def _rope(u, cmul, smul):
    return u * cmul + (pltpu.roll(u, 96, 1) + pltpu.roll(u, 32, 1)) * smul


def _kv_body(h_ref, wd_ref, lg_ref, wu_ref, kgn_ref, kgr_ref, c_ref, s_ref, k_ref, v_ref):
    low = _dot(h_ref[...], wd_ref[...])
    ckv = _rms(low[:, :KV_LORA_RANK], lg_ref[...]).astype(BF16)
    kv = _dot(ckv, wu_ref[...])
    pe = low[:, KV_LORA_RANK:]
    pe_ss = jnp.sum(pe * pe, axis=-1, keepdims=True)
    pe_rot = _rope(pe * kgr_ref[...], c_ref[...], s_ref[...])
    kgn = kgn_ref[...]
    for h in range(MLA_HEADS):
        kn = kv[:, h * HEAD_PAD:h * HEAD_PAD + MLA_NOPE_DIM]
        ms = (jnp.sum(kn * kn, axis=-1, keepdims=True) + pe_ss) * (1.0 / MLA_QK_DIM)
        rinv = lax.rsqrt(ms + EPS)
        k_ref[0, h, :, 0:MLA_NOPE_DIM] = ((kn * rinv) * kgn).astype(k_ref.dtype)
        k_ref[0, h, :, MLA_NOPE_DIM:HEAD_PAD] = (pe_rot * rinv).astype(k_ref.dtype)
        v_ref[0, h, :, 0:MLA_V_DIM] = kv[:, h * HEAD_PAD + MLA_NOPE_DIM:(h + 1) * HEAD_PAD].astype(v_ref.dtype)
        v_ref[0, h, :, MLA_V_DIM:V_EXT] = jnp.ones((kv.shape[0], V_EXT - MLA_V_DIM), v_ref.dtype)


def _shared_kv(h, w_dkv_pad, latent_gain, w_ukv, kg_nope, kg_rope_pad, cmul, smul, bsz):
    t, d = h.shape
    seq = t // bsz
    tps = seq // TM
    return pl.pallas_call(
        _kv_body,
        grid=(t // TM,),
        in_specs=[pl.BlockSpec((TM, d), lambda i: (i, 0)),
                  pl.BlockSpec((d, KV_DOWN_PAD), lambda i: (0, 0)),
                  pl.BlockSpec((1, KV_LORA_RANK), lambda i: (0, 0)),
                  pl.BlockSpec((KV_LORA_RANK, MLA_HEADS * HEAD_PAD), lambda i: (0, 0)),
                  pl.BlockSpec((1, MLA_NOPE_DIM), lambda i: (0, 0)),
                  pl.BlockSpec((1, LANES), lambda i: (0, 0)),
                  pl.BlockSpec((TM, LANES), lambda i: (i, 0)),
                  pl.BlockSpec((TM, LANES), lambda i: (i, 0))],
        out_specs=[pl.BlockSpec((1, MLA_HEADS, TM, HEAD_PAD), lambda i: (i // tps, 0, i % tps, 0)),
                   pl.BlockSpec((1, MLA_HEADS, TM, V_EXT), lambda i: (i // tps, 0, i % tps, 0))],
        out_shape=[jax.ShapeDtypeStruct((bsz, MLA_HEADS, seq, HEAD_PAD), BF16),
                   jax.ShapeDtypeStruct((bsz, MLA_HEADS, seq, V_EXT), BF16)],
        compiler_params=_params(("parallel",)),
        name="shared_kv",
    )(h, w_dkv_pad, latent_gain, w_ukv, kg_nope, kg_rope_pad, cmul, smul)


def _q_body(cq_ref, lg_ref, wu_ref, qg_ref, c_ref, s_ref, q_ref):
    cq = _rms(cq_ref[...], lg_ref[...], n=Q_LORA_RANK).astype(BF16)
    q = _dot(cq, wu_ref[...])
    qg = qg_ref[...]
    cmul = c_ref[...]
    smul = s_ref[...]
    c = (MLA_QK_DIM ** -0.5) * LOG2_E
    for h in range(MLA_HEADS):
        qh = _rms(q[:, h * HEAD_PAD:(h + 1) * HEAD_PAD], qg, n=MLA_QK_DIM)
        q_ref[0, h, :, 0:MLA_NOPE_DIM] = (qh[:, :MLA_NOPE_DIM] * c).astype(q_ref.dtype)
        q_ref[0, h, :, MLA_NOPE_DIM:HEAD_PAD] = (_rope(qh[:, MLA_NOPE_DIM:], cmul, smul) * c).astype(q_ref.dtype)


def _mla_q(cqmq, latent_gain_pad, w_uq_pad, q_gain_pad, cmul, smul, bsz):
    t = cqmq.shape[0]
    seq = t // bsz
    tps = seq // TM
    return pl.pallas_call(
        _q_body,
        grid=(t // TM,),
        in_specs=[pl.BlockSpec((TM, Q_LORA_PAD), lambda i: (i, 0)),
                  pl.BlockSpec((1, Q_LORA_PAD), lambda i: (0, 0)),
                  pl.BlockSpec((Q_LORA_PAD, MLA_HEADS * HEAD_PAD), lambda i: (0, 0)),
                  pl.BlockSpec((1, HEAD_PAD), lambda i: (0, 0)),
                  pl.BlockSpec((TM, LANES), lambda i: (i, 0)),
                  pl.BlockSpec((TM, LANES), lambda i: (i, 0))],
        out_specs=pl.BlockSpec((1, MLA_HEADS, TM, HEAD_PAD), lambda i: (i // tps, 0, i % tps, 0)),
        out_shape=jax.ShapeDtypeStruct((bsz, MLA_HEADS, seq, HEAD_PAD), BF16),
        compiler_params=_params(("parallel",)),
        name="mla_q",
    )(cqmq, latent_gain_pad, w_uq_pad, q_gain_pad, cmul, smul)


def _flash_body(qi_tab, ki_tab, last_tab, q_ref, k_ref, v_ref, o_ref, m_scr, acc_scr):
    p_id = pl.program_id(1)
    diag = qi_tab[p_id] == ki_tab[p_id]
    reps = TQ // LANES

    def scores(h):
        return _dot_nt(q_ref[0, h], k_ref[0, h])

    @pl.when(diag)
    def _():
        r_i = lax.broadcasted_iota(jnp.int32, (TQ, TQ), 0)
        c_i = lax.broadcasted_iota(jnp.int32, (TQ, TQ), 1)
        keep = c_i <= r_i
        for h in range(MLA_HEADS):
            s = jnp.where(keep, scores(h), -jnp.inf)
            m = jnp.broadcast_to(jnp.max(s, axis=-1, keepdims=True), (TQ, LANES))
            e = jnp.exp2(s - pltpu.repeat(m, reps, axis=1))
            m_scr[h] = m
            acc_scr[h] = _dot(e.astype(BF16), v_ref[0, h])

    @pl.when(jnp.logical_not(diag))
    def _():
        for h in range(MLA_HEADS):
            s = scores(h)
            m_prev = m_scr[h]
            m = jnp.maximum(m_prev, jnp.max(s, axis=-1, keepdims=True))
            alpha = jnp.exp2(m_prev - m)
            e = jnp.exp2(s - pltpu.repeat(m, reps, axis=1))
            m_scr[h] = m
            acc_scr[h] = pltpu.repeat(alpha, 2, axis=1) * acc_scr[h] + _dot(e.astype(BF16), v_ref[0, h])

    @pl.when(last_tab[p_id] == 1)
    def _():
        for h in range(MLA_HEADS):
            acc = acc_scr[h]
            o_ref[:, h * MLA_V_DIM:(h + 1) * MLA_V_DIM] = (
                acc[:, :MLA_V_DIM] * (1.0 / acc[:, MLA_V_DIM:])).astype(o_ref.dtype)


def _flash(q, k, v):
    bsz, nh, seq, _ = q.shape
    nq = seq // TQ
    qi, ki, last = [], [], []
    for a in range(nq):
        order = [a] + list(range(a))
        for n, b in enumerate(order):
            qi.append(a)
            ki.append(b)
            last.append(1 if n == len(order) - 1 else 0)
    npairs = len(qi)
    tabs = [jnp.asarray(x, jnp.int32) for x in (qi, ki, last)]
    grid_spec = pltpu.PrefetchScalarGridSpec(
        num_scalar_prefetch=3,
        grid=(bsz, npairs),
        in_specs=[pl.BlockSpec((1, nh, TQ, HEAD_PAD), lambda b, p, qt, kt, lt: (b, 0, qt[p], 0)),
                  pl.BlockSpec((1, nh, TQ, HEAD_PAD), lambda b, p, qt, kt, lt: (b, 0, kt[p], 0)),
                  pl.BlockSpec((1, nh, TQ, V_EXT), lambda b, p, qt, kt, lt: (b, 0, kt[p], 0))],
        out_specs=pl.BlockSpec((TQ, nh * MLA_V_DIM), lambda b, p, qt, kt, lt: (b * nq + qt[p], 0)),
        scratch_shapes=[pltpu.VMEM((nh, TQ, LANES), F32),
                        pltpu.VMEM((nh, TQ, V_EXT), F32)],
    )
    return pl.pallas_call(
        _flash_body,
        grid_spec=grid_spec,
        out_shape=jax.ShapeDtypeStruct((bsz * seq, nh * MLA_V_DIM), BF16),
        compiler_params=_params(("parallel", "arbitrary")),
        name="mla_attention",
    )(*tabs, q, k, v)


def _pad_heads(w, nh, dh, dh_pad):
    lead = w.shape[:-1]
    w = w.reshape(lead + (nh, dh))
    w = jnp.pad(w, [(0, 0)] * len(lead) + [(0, 0), (0, dh_pad - dh)])
    return w.reshape(lead + (nh * dh_pad,))


def _pad_last(w, n):
    return jnp.pad(w, [(0, 0)] * (w.ndim - 1) + [(0, n - w.shape[-1])])


def kernel(x, mem, positions, ffn1_gain, ffn1_w_in, ffn1_w_out, mix_gain, w_out, mem_gain, w_mem_kv, mem_q_gain, mem_k_gain, a_w_in, a_b_gates, a_conv, a_head_gain, kv_gain, w_dkv, kv_latent_gain, w_ukv, k_gain, b_w_in, b_q_latent_gain, b_w_uq, b_q_gain, ffn2_gain, ffn2_w_in, ffn2_w_out):
    bsz, seq, d = x.shape
    t = bsz * seq
    assert d == D_MODEL and seq % TM == 0 and seq % CHUNK == 0 and seq % TQ == 0 and t % min(TROPE, t) == 0

    ffn1_wi, ffn1_wo = ffn1_w_in.astype(BF16), ffn1_w_out.astype(BF16)
    ffn2_wi, ffn2_wo = ffn2_w_in.astype(BF16), ffn2_w_out.astype(BF16)
    w_out_b = w_out.astype(BF16)

    nqk = MLSTM_HEADS * MLSTM_QK_DIM
    c0, c1, c2, c3 = 2 * nqk, 2 * nqk + MAIN_WIDTH, 2 * nqk + 2 * MAIN_WIDTH, 2 * nqk + 2 * MAIN_WIDTH + 2 * MLSTM_HEADS

    cmul, smul = _rope_tables(positions)
    mk, mv = _mem_kv(mem.reshape(bsz * N_MEM, d), mem_gain, w_mem_kv.astype(BF16), mem_k_gain)

    xs = x.reshape(t, d)
    k_sh = v_sh = None
    for layer in range(DEPTH):
        xs, hmix = _ffn(xs, ffn1_gain[layer], ffn1_wi, ffn1_wo, layer, next_gain=mix_gain[layer])
        if layer < N_A_LAYERS:
            w_in = a_w_in[layer]
            w_qk = jnp.concatenate([_pad_heads(w_in[:, :nqk], MLSTM_HEADS, MLSTM_QK_DIM, QK_PAD),
                                    _pad_heads(w_in[:, nqk:c0], MLSTM_HEADS, MLSTM_QK_DIM, QK_PAD)], axis=-1)
            conv_w = jnp.concatenate([_pad_heads(a_conv[layer][:, :nqk], MLSTM_HEADS, MLSTM_QK_DIM, QK_PAD),
                                      _pad_heads(a_conv[layer][:, nqk:], MLSTM_HEADS, MLSTM_QK_DIM, QK_PAD)], axis=-1)
            w_g = w_in[:, c2:c3]
            qk = _matmul(hmix, w_qk.astype(BF16), F32, 1024)
            v = _matmul(hmix, w_in[:, c0:c1].astype(BF16), BF16, MAIN_WIDTH // 2)
            og = _matmul(hmix, w_in[:, c1:c2].astype(BF16), F32, MAIN_WIDTH // 2)
            mq_arr = _matmul(hmix, w_in[:, c3:].astype(BF16), F32, MEM_WIDTH)
            g, gt = _gates(hmix, _pad_last(w_g, GATE_PAD).astype(BF16),
                           jnp.pad(w_g.T, ((0, 2 * SUBLANES - 2 * MLSTM_HEADS), (0, 0))).astype(BF16),
                           a_b_gates[layer])
            main = _mlstm(qk, conv_w, v, og, g, gt, a_head_gain[layer], bsz)
            mq_block = 0
        else:
            j = layer - N_A_LAYERS
            w_in = b_w_in[j]
            w_b = jnp.concatenate([_pad_last(w_in[:, :Q_LORA_RANK], Q_LORA_PAD), w_in[:, Q_LORA_RANK:]], axis=-1)
            mq_arr = _matmul(hmix, w_b.astype(BF16), F32, Q_LORA_PAD)
            w_uq = jnp.pad(_pad_heads(b_w_uq[j], MLA_HEADS, MLA_QK_DIM, HEAD_PAD),
                           ((0, Q_LORA_PAD - Q_LORA_RANK), (0, 0))).astype(BF16)
            q = _mla_q(mq_arr, _pad_last(b_q_latent_gain[j], Q_LORA_PAD).reshape(1, Q_LORA_PAD), w_uq,
                       _pad_last(b_q_gain[j], HEAD_PAD).reshape(1, HEAD_PAD), cmul, smul, bsz)
            main = _flash(q, k_sh, v_sh)
            mq_block = 1
        xs = _mixer_out(xs, main, mq_arr, mq_block, mk, mv, mem_q_gain.reshape(DEPTH, 1, MEM_HEAD_DIM),
                        w_out_b, layer, seq)
        if layer == N_A_LAYERS - 1:
            xs, hkv = _ffn(xs, ffn2_gain[layer], ffn2_wi, ffn2_wo, layer, next_gain=kv_gain)
            k_sh, v_sh = _shared_kv(hkv, _pad_last(w_dkv, KV_DOWN_PAD).astype(BF16),
                                    kv_latent_gain.reshape(1, KV_LORA_RANK), w_ukv.astype(BF16),
                                    k_gain[:MLA_NOPE_DIM].reshape(1, MLA_NOPE_DIM),
                                    _pad_last(k_gain[MLA_NOPE_DIM:], LANES).reshape(1, LANES), cmul, smul, bsz)
        else:
            xs = _ffn(xs, ffn2_gain[layer], ffn2_wi, ffn2_wo, layer)
    return xs.reshape(bsz, seq, d)
```

```python
import functools

import jax
import jax.numpy as jnp
from jax import lax
from jax.experimental import pallas as pl
from jax.experimental.pallas import tpu as pltpu

F32 = jnp.float32
BF16 = jnp.bfloat16

D_MODEL = 2048
DEPTH = 4
N_MEM = 256
N_A_LAYERS = DEPTH // 2
EPS = 1e-6
MEM_HEADS = 4
MEM_WIDTH = D_MODEL // 4
MEM_HEAD_DIM = MEM_WIDTH // MEM_HEADS
MAIN_WIDTH = D_MODEL - MEM_WIDTH
MLSTM_HEADS = 4
MLSTM_V_DIM = MAIN_WIDTH // MLSTM_HEADS
MLSTM_QK_DIM = MLSTM_V_DIM // 2
CONV_WIDTH = 4
MLA_NOPE_DIM = 128
MLA_ROPE_DIM = 64
MLA_QK_DIM = MLA_NOPE_DIM + MLA_ROPE_DIM
MLA_V_DIM = 128
MLA_HEADS = MAIN_WIDTH // MLA_V_DIM
Q_LORA_RANK = 448
KV_LORA_RANK = 512
ROPE_THETA = 10000.0
D_FF = 5632
LOG2_E = 1.4426950408889634

LANES = 128
SUBLANES = 8
V7X_VMEM_BYTES = 64 * 1024 * 1024

QK_PAD = 256
Q_LORA_PAD = 512
HEAD_PAD = 256
GATE_PAD = LANES
KV_DOWN_PAD = KV_LORA_RANK + LANES
V_EXT = MLA_V_DIM + LANES

TM = 512
TM_FFN = 1024
TF = 512
CHUNK = 256
TQ = 512
TROPE = 2048
VMEM_LIMIT = 56 * 1024 * 1024


def _params(sem):
    return pltpu.CompilerParams(dimension_semantics=sem, vmem_limit_bytes=VMEM_LIMIT)


def _rms(x, gain, n=None):
    if n is None:
        ms = jnp.mean(x * x, axis=-1, keepdims=True)
    else:
        ms = jnp.sum(x * x, axis=-1, keepdims=True) * (1.0 / n)
    return (x * lax.rsqrt(ms + EPS)) * gain


def _log_sigmoid(x):
    return -(jnp.maximum(-x, 0.0) + jnp.log1p(jnp.exp(-jnp.abs(x))))


def _lane_tile(x, reps):
    return jnp.concatenate([x] * reps, axis=1)


def _dot(a, b):
    return jnp.dot(a, b, preferred_element_type=F32)


def _dot_nt(a, b):
    return lax.dot_general(a, b, (((1,), (1,)), ((), ())), preferred_element_type=F32)


def _dot_tn(a, b):
    return lax.dot_general(a, b, (((0,), (0,)), ((), ())), preferred_element_type=F32)


def _ffn_body(emit_next, x_hbm, g_ref, wg_ref, wu_ref, wo_ref, *rest):
    if emit_next:
        ng_ref, o_ref, hn_ref, h_scr, x_buf, x_sem = rest
    else:
        o_ref, h_scr, x_buf, x_sem = rest
    i = pl.program_id(0)
    j = pl.program_id(1)

    def x_copy(tile):
        rows = pl.ds(pl.multiple_of(tile * TM_FFN, TM_FFN), TM_FFN)
        return pltpu.make_async_copy(x_hbm.at[rows, :], x_buf, x_sem)

    @pl.when(j == 0)
    def _():
        @pl.when(i == 0)
        def _():
            x_copy(0).start()

        x_copy(i).wait()
        x = x_buf[...]
        h_scr[...] = _rms(x, g_ref[...]).astype(BF16)
        o_ref[...] = x

    @pl.when(jnp.logical_and(j == 1, i + 1 < pl.num_programs(0)))
    def _():
        x_copy(i + 1).start()

    h = h_scr[...]
    g = _dot(h, wg_ref[...])
    u = _dot(h, wu_ref[...])
    a = ((g * jax.nn.sigmoid(g)) * u) * 0.5
    o_ref[...] += _dot(a.astype(BF16), wo_ref[...])

    if emit_next:
        @pl.when(j == pl.num_programs(1) - 1)
        def _():
            hn_ref[...] = _rms(o_ref[...], ng_ref[...]).astype(BF16)


def _ffn(x, gain, w_in, w_out, layer, next_gain=None):
    t, d = x.shape
    nf = D_FF // TF
    assert nf >= 2 and t % TM_FFN == 0
    emit = next_gain is not None
    in_specs = [
        pl.BlockSpec(memory_space=pl.ANY),
        pl.BlockSpec((1, d), lambda i, j: (0, 0)),
        pl.BlockSpec((None, d, TF), lambda i, j: (layer, 0, j)),
        pl.BlockSpec((None, d, TF), lambda i, j: (layer, 0, j + nf)),
        pl.BlockSpec((None, TF, d), lambda i, j: (layer, j, 0)),
    ]
    args = [x, gain.reshape(1, d), w_in, w_in, w_out]
    out_shape = [jax.ShapeDtypeStruct((t, d), F32)]
    out_specs = [pl.BlockSpec((TM_FFN, d), lambda i, j: (i, 0))]
    if emit:
        in_specs.append(pl.BlockSpec((1, d), lambda i, j: (0, 0)))
        args.append(next_gain.reshape(1, d))
        out_shape.append(jax.ShapeDtypeStruct((t, d), BF16))
        out_specs.append(pl.BlockSpec((TM_FFN, d), lambda i, j: (i, 0)))
    res = pl.pallas_call(
        functools.partial(_ffn_body, emit),
        grid=(t // TM_FFN, nf),
        in_specs=in_specs,
        out_specs=out_specs,
        out_shape=out_shape,
        scratch_shapes=[pltpu.VMEM((TM_FFN, d), BF16),
                        pltpu.VMEM((TM_FFN, d), F32),
                        pltpu.SemaphoreType.DMA(())],
        compiler_params=_params(("arbitrary", "arbitrary")),
        name="ffn_next" if emit else "ffn",
    )(*args)
    return (res[0], res[1]) if emit else res[0]


def _resident(shape):
    return pl.BlockSpec(shape, lambda i: (0,) * len(shape), pipeline_mode=pl.Buffered(1))


def _proja_body(h_ref, wqk_ref, wv_ref, wo_ref, wmq_ref, wg_ref, wgt_ref, brow_ref, bcol_ref,
                qk_ref, v_ref, og_ref, mq_ref, g_ref, gt_ref):
    h = h_ref[...]
    qk_ref[...] = _dot(h, wqk_ref[...])
    v_ref[...] = _dot(h, wv_ref[...]).astype(v_ref.dtype)
    og_ref[...] = _dot(h, wo_ref[...])
    mq_ref[...] = _dot(h, wmq_ref[...])
    g = _dot(h, wg_ref[...]) + brow_ref[...]
    lane = lax.broadcasted_iota(jnp.int32, g.shape, 1)
    g_ref[...] = jnp.where(lane >= MLSTM_HEADS, _log_sigmoid(g), g)
    gt = _dot_nt(wgt_ref[...], h)[:2 * MLSTM_HEADS] + bcol_ref[...]
    row = lax.broadcasted_iota(jnp.int32, gt.shape, 0)
    gt_ref[...] = jnp.where(row >= MLSTM_HEADS, _log_sigmoid(gt), gt)


def _proj_a(h, w_qk, w_v, w_o, w_mq, w_g, w_gt, bias):
    t, d = h.shape
    ng = 2 * MLSTM_HEADS
    nqk = w_qk.shape[1]
    brow = jnp.zeros((1, GATE_PAD), F32).at[0, :ng].set(bias)
    bcol = bias.reshape(ng, 1)
    row_spec = lambda n: pl.BlockSpec((TM, n), lambda i: (i, 0))
    return pl.pallas_call(
        _proja_body,
        grid=(t // TM,),
        in_specs=[row_spec(d), _resident((d, nqk)), _resident((d, MAIN_WIDTH)), _resident((d, MAIN_WIDTH)),
                  _resident((d, MEM_WIDTH)), _resident((d, GATE_PAD)), _resident((2 * SUBLANES, d)),
                  _resident((1, GATE_PAD)), _resident((ng, 1))],
        out_specs=[row_spec(nqk), row_spec(MAIN_WIDTH), row_spec(MAIN_WIDTH), row_spec(MEM_WIDTH),
                   row_spec(GATE_PAD), pl.BlockSpec((ng, TM), lambda i: (0, i))],
        out_shape=[jax.ShapeDtypeStruct((t, nqk), F32),
                   jax.ShapeDtypeStruct((t, MAIN_WIDTH), BF16),
                   jax.ShapeDtypeStruct((t, MAIN_WIDTH), F32),
                   jax.ShapeDtypeStruct((t, MEM_WIDTH), F32),
                   jax.ShapeDtypeStruct((t, GATE_PAD), F32),
                   jax.ShapeDtypeStruct((ng, t), F32)],
        compiler_params=_params(("parallel",)),
        name="proj_mlstm",
    )(h, w_qk, w_v, w_o, w_mq, w_g, w_gt, brow, bcol)


def _mlstm_body(qk_ref, cw_ref, v_ref, og_ref, g_ref, gt_ref, hg_ref, out_ref, xbuf, c_scr, m_scr):
    nh, dkp, dv = MLSTM_HEADS, QK_PAD, MLSTM_V_DIM
    ell = CHUNK
    halo = SUBLANES

    @pl.when(pl.program_id(1) == 0)
    def _():
        xbuf[0:halo, :] = jnp.zeros((halo, xbuf.shape[1]), F32)
        c_scr[...] = jnp.zeros(c_scr.shape, F32)
        m_scr[...] = jnp.zeros(m_scr.shape, F32)

    xbuf[halo:halo + ell, :] = qk_ref[...]
    cw = cw_ref[...]
    y = cw[CONV_WIDTH - 1:CONV_WIDTH, :] * xbuf[halo:halo + ell, :]
    for j in range(CONV_WIDTH - 1):
        sh = CONV_WIDTH - 1 - j
        y = y + cw[j:j + 1, :] * xbuf[halo - sh:halo - sh + ell, :]
    xbuf[0:halo, :] = xbuf[ell:ell + halo, :]
    y = y * jax.nn.sigmoid(y)

    r_i = lax.broadcasted_iota(jnp.int32, (ell, ell), 0)
    c_i = lax.broadcasted_iota(jnp.int32, (ell, ell), 1)
    causal = c_i <= r_i
    tri = jnp.where(causal, 1.0, 0.0).astype(F32)
    tri_t = jnp.where(r_i <= c_i, 1.0, 0.0).astype(F32)
    gcol = g_ref[...]
    grow = gt_ref[...]
    bcol_all = jnp.dot(tri, gcol, preferred_element_type=F32, precision=lax.Precision.HIGHEST)
    brow_all = jnp.dot(grow, tri_t, preferred_element_type=F32, precision=lax.Precision.HIGHEST)

    lane = lax.broadcasted_iota(jnp.int32, (ell, LANES), 1)
    ones_col = jnp.where(lane == 0, 1.0, 0.0).astype(BF16)
    v_all = v_ref[...]
    og = og_ref[...]
    hg = hg_ref[...]

    for h in range(nh):
        q = (y[:, h * dkp:(h + 1) * dkp] * (MLSTM_QK_DIM ** -0.5)).astype(BF16)
        k_f = y[:, (nh + h) * dkp:(nh + h + 1) * dkp]
        k = k_f.astype(BF16)
        v_ext = jnp.concatenate([v_all[:, h * dv:(h + 1) * dv], ones_col], axis=-1)
        ig_col = gcol[:, h:h + 1]
        ig_row = grow[h:h + 1, :]
        b_col = bcol_all[:, nh + h:nh + h + 1]
        b_row = brow_all[nh + h:nh + h + 1, :]
        m_prev = m_scr[h][:, 0:1]
        c_prev = c_scr[h]

        log_w = jnp.where(causal, b_col - b_row + ig_row, -jnp.inf)
        log_inter = b_col + m_prev
        m_t = jnp.maximum(jnp.max(log_w, axis=-1, keepdims=True), log_inter)
        w = jnp.exp(log_w - m_t)
        w_inter = jnp.exp(log_inter - m_t)
        s = _dot_nt(q, k) * w
        inter = _dot(q, c_prev.astype(BF16))
        num = _dot(s.astype(BF16), v_ext[:, :dv]) + w_inter * inter[:, :dv]
        den = jnp.sum(s, axis=-1, keepdims=True) + w_inter * inter[:, dv:dv + 1]
        hh = num * (1.0 / jnp.maximum(jnp.abs(den), jnp.exp(-m_t)))

        g_last = b_col[ell - 1:ell, :]
        log_a = g_last - b_col + ig_col
        m_new = jnp.maximum(g_last + m_prev, jnp.max(log_a, axis=0, keepdims=True))
        decay = jnp.exp(g_last + m_prev - m_new)
        a = jnp.exp(log_a - m_new)
        c_scr[h] = decay * c_prev + _dot_tn((k_f * a).astype(BF16), v_ext)
        m_scr[h] = jnp.broadcast_to(m_new, (1, LANES))

        hn = _rms(hh, hg[:, h * dv:(h + 1) * dv])
        o_h = og[:, h * dv:(h + 1) * dv]
        out_ref[:, h * dv:(h + 1) * dv] = (jax.nn.sigmoid(o_h) * hn).astype(out_ref.dtype)


def _mlstm(qk, conv_w, v, og, g, gt, head_gain, bsz):
    t = qk.shape[0]
    nc = t // bsz // CHUNK
    nqk = qk.shape[1]
    return pl.pallas_call(
        _mlstm_body,
        grid=(bsz, nc),
        in_specs=[pl.BlockSpec((CHUNK, nqk), lambda b, c: (b * nc + c, 0)),
                  pl.BlockSpec((CONV_WIDTH, nqk), lambda b, c: (0, 0)),
                  pl.BlockSpec((CHUNK, MAIN_WIDTH), lambda b, c: (b * nc + c, 0)),
                  pl.BlockSpec((CHUNK, MAIN_WIDTH), lambda b, c: (b * nc + c, 0)),
                  pl.BlockSpec((CHUNK, GATE_PAD), lambda b, c: (b * nc + c, 0)),
                  pl.BlockSpec((2 * MLSTM_HEADS, CHUNK), lambda b, c: (0, b * nc + c)),
                  pl.BlockSpec((1, MAIN_WIDTH), lambda b, c: (0, 0))],
        out_specs=pl.BlockSpec((CHUNK, MAIN_WIDTH), lambda b, c: (b * nc + c, 0)),
        out_shape=jax.ShapeDtypeStruct((t, MAIN_WIDTH), BF16),
        scratch_shapes=[pltpu.VMEM((CHUNK + SUBLANES, nqk), F32),
                        pltpu.VMEM((MLSTM_HEADS, QK_PAD, MLSTM_V_DIM + LANES), F32),
                        pltpu.VMEM((MLSTM_HEADS, 1, LANES), F32)],
        compiler_params=_params(("parallel", "arbitrary")),
        name="mlstm",
    )(qk, conv_w, v, og, g, gt, head_gain.reshape(1, MAIN_WIDTH))


def _memkv_body(mem_ref, g_ref, w_ref, kg_ref, mk_ref, mv_ref):
    hn = _rms(mem_ref[...], g_ref[...]).astype(BF16)
    kv = _dot(hn, w_ref[...])
    kg = kg_ref[...]
    for h in range(MEM_HEADS):
        sl = slice(h * MEM_HEAD_DIM, (h + 1) * MEM_HEAD_DIM)
        mk_ref[:, sl] = _rms(kv[:, sl], kg).astype(mk_ref.dtype)
    mv_ref[...] = kv[:, MEM_WIDTH:].astype(mv_ref.dtype)


def _mem_kv(mem2d, mem_gain, w_mem_kv, mem_k_gain):
    rows = mem2d.shape[0]
    shp = jax.ShapeDtypeStruct((DEPTH, rows, MEM_WIDTH), BF16)
    return pl.pallas_call(
        _memkv_body,
        grid=(DEPTH,),
        in_specs=[pl.BlockSpec((rows, D_MODEL), lambda l: (0, 0)),
                  pl.BlockSpec((None, 1, D_MODEL), lambda l: (l, 0, 0)),
                  pl.BlockSpec((None, D_MODEL, 2 * MEM_WIDTH), lambda l: (l, 0, 0)),
                  pl.BlockSpec((None, 1, MEM_HEAD_DIM), lambda l: (l, 0, 0))],
        out_specs=[pl.BlockSpec((None, rows, MEM_WIDTH), lambda l: (l, 0, 0)),
                   pl.BlockSpec((None, rows, MEM_WIDTH), lambda l: (l, 0, 0))],
        out_shape=[shp, shp],
        compiler_params=_params(("parallel",)),
        name="mem_kv",
    )(mem2d, mem_gain.reshape(DEPTH, 1, D_MODEL), w_mem_kv, mem_k_gain.reshape(DEPTH, 1, MEM_HEAD_DIM))


def _mix_body(x_ref, main_ref, mq_ref, mk_ref, mv_ref, qg_ref, wmain_ref, wmem_ref, o_ref):
    mq = mq_ref[...]
    mk = mk_ref[...]
    mv = mv_ref[...]
    qg = qg_ref[...]
    heads = []
    for h in range(MEM_HEADS):
        sl = slice(h * MEM_HEAD_DIM, (h + 1) * MEM_HEAD_DIM)
        qn = _rms(mq[:, sl], qg).astype(BF16)
        s = _dot_nt(qn, mk[:, sl]) * (MEM_HEAD_DIM ** -0.5)
        e = jnp.exp(s - jnp.max(s, axis=-1, keepdims=True))
        oh = _dot(e.astype(BF16), mv[:, sl]) * (1.0 / jnp.sum(e, axis=-1, keepdims=True))
        heads.append(oh.astype(BF16))
    mem_out = jnp.concatenate(heads, axis=-1)
    o_ref[...] = x_ref[...] + _dot(main_ref[...], wmain_ref[...]) + _dot(mem_out, wmem_ref[...])


def _mixer_out(x, main, mq, mk, mv, q_gain, w_out, layer, seq):
    t, d = x.shape
    tiles_per_seq = seq // TM
    return pl.pallas_call(
        _mix_body,
        grid=(t // TM,),
        in_specs=[pl.BlockSpec((TM, d), lambda i: (i, 0)),
                  pl.BlockSpec((TM, MAIN_WIDTH), lambda i: (i, 0)),
                  pl.BlockSpec((TM, MEM_WIDTH), lambda i: (i, 0)),
                  pl.BlockSpec((None, N_MEM, MEM_WIDTH), lambda i: (layer, i // tiles_per_seq, 0)),
                  pl.BlockSpec((None, N_MEM, MEM_WIDTH), lambda i: (layer, i // tiles_per_seq, 0)),
                  pl.BlockSpec((None, 1, MEM_HEAD_DIM), lambda i: (layer, 0, 0)),
                  pl.BlockSpec((None, MAIN_WIDTH, d), lambda i: (layer, 0, 0)),
                  pl.BlockSpec((None, MEM_WIDTH, d), lambda i: (layer, MAIN_WIDTH // MEM_WIDTH, 0))],
        out_specs=pl.BlockSpec((TM, d), lambda i: (i, 0)),
        out_shape=jax.ShapeDtypeStruct((t, d), F32),
        compiler_params=_params(("parallel",)),
        name="mixer_out",
    )(x, main, mq, mk, mv, q_gain, w_out, w_out)


def _rope_body(pos_ref, freq_ref, c_ref, s_ref):
    ang = pos_ref[...].astype(F32) * freq_ref[...]
    lane = lax.broadcasted_iota(jnp.int32, ang.shape, 1)
    half = MLA_ROPE_DIM // 2
    cos = jnp.cos(ang)
    sin = jnp.sin(ang)
    c_ref[...] = jnp.where(lane < MLA_ROPE_DIM, cos, 0.0)
    s_ref[...] = jnp.where(lane < half, -sin, jnp.where(lane < MLA_ROPE_DIM, sin, 0.0))


def _rope_tables(positions):
    t = positions.size
    tr = min(TROPE, t)
    half = MLA_ROPE_DIM // 2
    inv_freq = ROPE_THETA ** (-jnp.arange(0, MLA_ROPE_DIM, 2, dtype=F32) / MLA_ROPE_DIM)
    freq = jnp.tile(inv_freq, LANES // half).reshape(1, LANES)
    shp = jax.ShapeDtypeStruct((t, LANES), F32)
    return pl.pallas_call(
        _rope_body,
        grid=(t // tr,),
        in_specs=[pl.BlockSpec((tr, 1), lambda i: (i, 0)),
                  pl.BlockSpec((1, LANES), lambda i: (0, 0))],
        out_specs=[pl.BlockSpec((tr, LANES), lambda i: (i, 0)),
                   pl.BlockSpec((tr, LANES), lambda i: (i, 0))],
        out_shape=[shp, shp],
        compiler_params=_params(("parallel",)),
        name="rope_tables",
    )(positions.reshape(t, 1), freq)


def _rope(u, cmul, smul):
    return u * cmul + (pltpu.roll(u, 96, 1) + pltpu.roll(u, 32, 1)) * smul


def _kv_body(h_ref, wd_ref, lg_ref, wu_ref, kgn_ref, kgr_ref, c_ref, s_ref, k_ref, v_ref):
    low = _dot(h_ref[...], wd_ref[...])
    ckv = _rms(low[:, :KV_LORA_RANK], lg_ref[...]).astype(BF16)
    kv = _dot(ckv, wu_ref[...])
    pe = low[:, KV_LORA_RANK:]
    pe_ss = jnp.sum(pe * pe, axis=-1, keepdims=True)
    pe_rot = _rope(pe * kgr_ref[...], c_ref[...], s_ref[...])
    kgn = kgn_ref[...]
    for h in range(MLA_HEADS):
        kn = kv[:, h * HEAD_PAD:h * HEAD_PAD + MLA_NOPE_DIM]
        ms = (jnp.sum(kn * kn, axis=-1, keepdims=True) + pe_ss) * (1.0 / MLA_QK_DIM)
        rinv = lax.rsqrt(ms + EPS)
        k_ref[0, h, :, 0:MLA_NOPE_DIM] = ((kn * rinv) * kgn).astype(k_ref.dtype)
        k_ref[0, h, :, MLA_NOPE_DIM:HEAD_PAD] = (pe_rot * rinv).astype(k_ref.dtype)
        v_ref[0, h, :, 0:MLA_V_DIM] = kv[:, h * HEAD_PAD + MLA_NOPE_DIM:(h + 1) * HEAD_PAD].astype(v_ref.dtype)
        v_ref[0, h, :, MLA_V_DIM:V_EXT] = jnp.ones((kv.shape[0], V_EXT - MLA_V_DIM), v_ref.dtype)


def _shared_kv(h, w_dkv_pad, latent_gain, w_ukv, kg_nope, kg_rope_pad, cmul, smul, bsz):
    t, d = h.shape
    seq = t // bsz
    tps = seq // TM
    return pl.pallas_call(
        _kv_body,
        grid=(t // TM,),
        in_specs=[pl.BlockSpec((TM, d), lambda i: (i, 0)),
                  pl.BlockSpec((d, KV_DOWN_PAD), lambda i: (0, 0)),
                  pl.BlockSpec((1, KV_LORA_RANK), lambda i: (0, 0)),
                  pl.BlockSpec((KV_LORA_RANK, MLA_HEADS * HEAD_PAD), lambda i: (0, 0)),
                  pl.BlockSpec((1, MLA_NOPE_DIM), lambda i: (0, 0)),
                  pl.BlockSpec((1, LANES), lambda i: (0, 0)),
                  pl.BlockSpec((TM, LANES), lambda i: (i, 0)),
                  pl.BlockSpec((TM, LANES), lambda i: (i, 0))],
        out_specs=[pl.BlockSpec((1, MLA_HEADS, TM, HEAD_PAD), lambda i: (i // tps, 0, i % tps, 0)),
                   pl.BlockSpec((1, MLA_HEADS, TM, V_EXT), lambda i: (i // tps, 0, i % tps, 0))],
        out_shape=[jax.ShapeDtypeStruct((bsz, MLA_HEADS, seq, HEAD_PAD), BF16),
                   jax.ShapeDtypeStruct((bsz, MLA_HEADS, seq, V_EXT), BF16)],
        compiler_params=_params(("parallel",)),
        name="shared_kv",
    )(h, w_dkv_pad, latent_gain, w_ukv, kg_nope, kg_rope_pad, cmul, smul)


def _q_body(h_ref, wb_ref, lg_ref, wu_ref, qg_ref, c_ref, s_ref, mq_ref, q_ref):
    low = _dot(h_ref[...], wb_ref[...])
    mq_ref[...] = low[:, Q_LORA_PAD:]
    cq = _rms(low[:, :Q_LORA_PAD], lg_ref[...], n=Q_LORA_RANK).astype(BF16)
    q = _dot(cq, wu_ref[...])
    qg = qg_ref[...]
    cmul = c_ref[...]
    smul = s_ref[...]
    c = (MLA_QK_DIM ** -0.5) * LOG2_E
    for h in range(MLA_HEADS):
        qh = _rms(q[:, h * HEAD_PAD:(h + 1) * HEAD_PAD], qg, n=MLA_QK_DIM)
        q_ref[0, h, :, 0:MLA_NOPE_DIM] = (qh[:, :MLA_NOPE_DIM] * c).astype(q_ref.dtype)
        q_ref[0, h, :, MLA_NOPE_DIM:HEAD_PAD] = (_rope(qh[:, MLA_NOPE_DIM:], cmul, smul) * c).astype(q_ref.dtype)


def _mla_q(h, w_b_pad, latent_gain_pad, w_uq_pad, q_gain_pad, cmul, smul, bsz):
    t, d = h.shape
    seq = t // bsz
    tps = seq // TM
    return pl.pallas_call(
        _q_body,
        grid=(t // TM,),
        in_specs=[pl.BlockSpec((TM, d), lambda i: (i, 0)),
                  _resident((d, Q_LORA_PAD + MEM_WIDTH)),
                  _resident((1, Q_LORA_PAD)),
                  _resident((Q_LORA_PAD, MLA_HEADS * HEAD_PAD)),
                  _resident((1, HEAD_PAD)),
                  pl.BlockSpec((TM, LANES), lambda i: (i, 0)),
                  pl.BlockSpec((TM, LANES), lambda i: (i, 0))],
        out_specs=[pl.BlockSpec((TM, MEM_WIDTH), lambda i: (i, 0)),
                   pl.BlockSpec((1, MLA_HEADS, TM, HEAD_PAD), lambda i: (i // tps, 0, i % tps, 0))],
        out_shape=[jax.ShapeDtypeStruct((t, MEM_WIDTH), F32),
                   jax.ShapeDtypeStruct((bsz, MLA_HEADS, seq, HEAD_PAD), BF16)],
        compiler_params=_params(("parallel",)),
        name="proj_mla_q",
    )(h, w_b_pad, latent_gain_pad, w_uq_pad, q_gain_pad, cmul, smul)


def _flash_body(qi_tab, ki_tab, last_tab, q_ref, k_ref, v_ref, o_ref, m_scr, acc_scr):
    p_id = pl.program_id(1)
    diag = qi_tab[p_id] == ki_tab[p_id]
    reps = TQ // LANES

    def scores(h):
        return _dot_nt(q_ref[0, h], k_ref[0, h])

    @pl.when(diag)
    def _():
        r_i = lax.broadcasted_iota(jnp.int32, (TQ, TQ), 0)
        c_i = lax.broadcasted_iota(jnp.int32, (TQ, TQ), 1)
        keep = c_i <= r_i
        for h in range(MLA_HEADS):
            s = jnp.where(keep, scores(h), -jnp.inf)
            m = jnp.broadcast_to(jnp.max(s, axis=-1, keepdims=True), (TQ, LANES))
            e = jnp.exp2(s - _lane_tile(m, reps))
            m_scr[h] = m
            acc_scr[h] = _dot(e.astype(BF16), v_ref[0, h])

    @pl.when(jnp.logical_not(diag))
    def _():
        for h in range(MLA_HEADS):
            s = scores(h)
            m_prev = m_scr[h]
            m = jnp.maximum(m_prev, jnp.max(s, axis=-1, keepdims=True))
            alpha = jnp.exp2(m_prev - m)
            e = jnp.exp2(s - _lane_tile(m, reps))
            m_scr[h] = m
            acc_scr[h] = _lane_tile(alpha, V_EXT // LANES) * acc_scr[h] + _dot(e.astype(BF16), v_ref[0, h])

    @pl.when(last_tab[p_id] == 1)
    def _():
        for h in range(MLA_HEADS):
            acc = acc_scr[h]
            o_ref[:, h * MLA_V_DIM:(h + 1) * MLA_V_DIM] = (
                acc[:, :MLA_V_DIM] * (1.0 / acc[:, MLA_V_DIM:])).astype(o_ref.dtype)


def _flash(q, k, v):
    bsz, nh, seq, _ = q.shape
    nq = seq // TQ
    qi, ki, last = [], [], []
    for a in range(nq):
        order = [a] + list(range(a))
        for n, b in enumerate(order):
            qi.append(a)
            ki.append(b)
            last.append(1 if n == len(order) - 1 else 0)
    npairs = len(qi)
    tabs = [jnp.asarray(x, jnp.int32) for x in (qi, ki, last)]
    grid_spec = pltpu.PrefetchScalarGridSpec(
        num_scalar_prefetch=3,
        grid=(bsz, npairs),
        in_specs=[pl.BlockSpec((1, nh, TQ, HEAD_PAD), lambda b, p, qt, kt, lt: (b, 0, qt[p], 0)),
                  pl.BlockSpec((1, nh, TQ, HEAD_PAD), lambda b, p, qt, kt, lt: (b, 0, kt[p], 0)),
                  pl.BlockSpec((1, nh, TQ, V_EXT), lambda b, p, qt, kt, lt: (b, 0, kt[p], 0))],
        out_specs=pl.BlockSpec((TQ, nh * MLA_V_DIM), lambda b, p, qt, kt, lt: (b * nq + qt[p], 0)),
        scratch_shapes=[pltpu.VMEM((nh, TQ, LANES), F32),
                        pltpu.VMEM((nh, TQ, V_EXT), F32)],
    )
    return pl.pallas_call(
        _flash_body,
        grid_spec=grid_spec,
        out_shape=jax.ShapeDtypeStruct((bsz * seq, nh * MLA_V_DIM), BF16),
        compiler_params=_params(("parallel", "arbitrary")),
        name="mla_attention",
    )(*tabs, q, k, v)


def _pad_heads(w, nh, dh, dh_pad):
    lead = w.shape[:-1]
    w = w.reshape(lead + (nh, dh))
    w = jnp.pad(w, [(0, 0)] * len(lead) + [(0, 0), (0, dh_pad - dh)])
    return w.reshape(lead + (nh * dh_pad,))


def _pad_last(w, n):
    return jnp.pad(w, [(0, 0)] * (w.ndim - 1) + [(0, n - w.shape[-1])])


def kernel(x, mem, positions, ffn1_gain, ffn1_w_in, ffn1_w_out, mix_gain, w_out, mem_gain, w_mem_kv, mem_q_gain, mem_k_gain, a_w_in, a_b_gates, a_conv, a_head_gain, kv_gain, w_dkv, kv_latent_gain, w_ukv, k_gain, b_w_in, b_q_latent_gain, b_w_uq, b_q_gain, ffn2_gain, ffn2_w_in, ffn2_w_out):
    bsz, seq, d = x.shape
    t = bsz * seq
    assert d == D_MODEL and seq % TM == 0 and seq % CHUNK == 0 and seq % TQ == 0 and t % min(TROPE, t) == 0 and t % TM_FFN == 0

    ffn1_wi, ffn1_wo = ffn1_w_in.astype(BF16), ffn1_w_out.astype(BF16)
    ffn2_wi, ffn2_wo = ffn2_w_in.astype(BF16), ffn2_w_out.astype(BF16)
    w_out_b = w_out.astype(BF16)

    nqk = MLSTM_HEADS * MLSTM_QK_DIM
    c0, c1, c2, c3 = 2 * nqk, 2 * nqk + MAIN_WIDTH, 2 * nqk + 2 * MAIN_WIDTH, 2 * nqk + 2 * MAIN_WIDTH + 2 * MLSTM_HEADS

    cmul, smul = _rope_tables(positions)
    mk, mv = _mem_kv(mem.reshape(bsz * N_MEM, d), mem_gain, w_mem_kv.astype(BF16), mem_k_gain)

    xs = x.reshape(t, d)
    k_sh = v_sh = None
    for layer in range(DEPTH):
        xs, hmix = _ffn(xs, ffn1_gain[layer], ffn1_wi, ffn1_wo, layer, next_gain=mix_gain[layer])
        if layer < N_A_LAYERS:
            w_in = a_w_in[layer]
            w_qk = jnp.concatenate([_pad_heads(w_in[:, :nqk], MLSTM_HEADS, MLSTM_QK_DIM, QK_PAD),
                                    _pad_heads(w_in[:, nqk:c0], MLSTM_HEADS, MLSTM_QK_DIM, QK_PAD)], axis=-1)
            conv_w = jnp.concatenate([_pad_heads(a_conv[layer][:, :nqk], MLSTM_HEADS, MLSTM_QK_DIM, QK_PAD),
                                      _pad_heads(a_conv[layer][:, nqk:], MLSTM_HEADS, MLSTM_QK_DIM, QK_PAD)], axis=-1)
            w_g = w_in[:, c2:c3]
            qk, v, og, mq, g, gt = _proj_a(
                hmix, w_qk.astype(BF16), w_in[:, c0:c1].astype(BF16), w_in[:, c1:c2].astype(BF16),
                w_in[:, c3:].astype(BF16), _pad_last(w_g, GATE_PAD).astype(BF16),
                jnp.pad(w_g.T, ((0, 2 * SUBLANES - 2 * MLSTM_HEADS), (0, 0))).astype(BF16), a_b_gates[layer])
            main = _mlstm(qk, conv_w, v, og, g, gt, a_head_gain[layer], bsz)
        else:
            j = layer - N_A_LAYERS
            w_in = b_w_in[j]
            w_b = jnp.concatenate([_pad_last(w_in[:, :Q_LORA_RANK], Q_LORA_PAD), w_in[:, Q_LORA_RANK:]], axis=-1)
            w_uq = jnp.pad(_pad_heads(b_w_uq[j], MLA_HEADS, MLA_QK_DIM, HEAD_PAD),
                           ((0, Q_LORA_PAD - Q_LORA_RANK), (0, 0))).astype(BF16)
            mq, q = _mla_q(hmix, w_b.astype(BF16), _pad_last(b_q_latent_gain[j], Q_LORA_PAD).reshape(1, Q_LORA_PAD),
                           w_uq, _pad_last(b_q_gain[j], HEAD_PAD).reshape(1, HEAD_PAD), cmul, smul, bsz)
            main = _flash(q, k_sh, v_sh)
        xs = _mixer_out(xs, main, mq, mk, mv, mem_q_gain.reshape(DEPTH, 1, MEM_HEAD_DIM), w_out_b, layer, seq)
        if layer == N_A_LAYERS - 1:
            xs, hkv = _ffn(xs, ffn2_gain[layer], ffn2_wi, ffn2_wo, layer, next_gain=kv_gain)
            k_sh, v_sh = _shared_kv(hkv, _pad_last(w_dkv, KV_DOWN_PAD).astype(BF16),
                                    kv_latent_gain.reshape(1, KV_LORA_RANK), w_ukv.astype(BF16),
                                    k_gain[:MLA_NOPE_DIM].reshape(1, MLA_NOPE_DIM),
                                    _pad_last(k_gain[MLA_NOPE_DIM:], LANES).reshape(1, LANES), cmul, smul, bsz)
        else:
            xs = _ffn(xs, ffn2_gain[layer], ffn2_wi, ffn2_wo, layer)
    return xs.reshape(bsz, seq, d)
```

```python
import functools

import jax
import jax.numpy as jnp
from jax import lax
from jax.experimental import pallas as pl
from jax.experimental.pallas import tpu as pltpu

F32 = jnp.float32
BF16 = jnp.bfloat16

D_MODEL = 2048
DEPTH = 4
N_MEM = 256
N_A_LAYERS = DEPTH // 2
EPS = 1e-6
MEM_HEADS = 4
MEM_WIDTH = D_MODEL // 4
MEM_HEAD_DIM = MEM_WIDTH // MEM_HEADS
MAIN_WIDTH = D_MODEL - MEM_WIDTH
MLSTM_HEADS = 4
MLSTM_V_DIM = MAIN_WIDTH // MLSTM_HEADS
MLSTM_QK_DIM = MLSTM_V_DIM // 2
CONV_WIDTH = 4
MLA_NOPE_DIM = 128
MLA_ROPE_DIM = 64
MLA_QK_DIM = MLA_NOPE_DIM + MLA_ROPE_DIM
MLA_V_DIM = 128
MLA_HEADS = MAIN_WIDTH // MLA_V_DIM
Q_LORA_RANK = 448
KV_LORA_RANK = 512
ROPE_THETA = 10000.0
D_FF = 5632
LOG2_E = 1.4426950408889634

LANES = 128
SUBLANES = 8
V7X_VMEM_BYTES = 64 * 1024 * 1024

QK_PAD = 256
Q_LORA_PAD = 512
HEAD_PAD = 256
GATE_PAD = LANES
KV_DOWN_PAD = KV_LORA_RANK + LANES
V_EXT = MLA_V_DIM + LANES

TM = 512
TM_FFN = 1024
TF = 512
CHUNK = 256
TQ = 512
TROPE = 2048
MIB = 1024 * 1024
VMEM_LIMIT = 56 * MIB
VMEM_LIMIT_FFN = 58 * MIB
assert VMEM_LIMIT_FFN < V7X_VMEM_BYTES


def _params(sem, vmem_limit=VMEM_LIMIT):
    return pltpu.CompilerParams(dimension_semantics=sem, vmem_limit_bytes=vmem_limit)


def _rms(x, gain, n=None):
    if n is None:
        ms = jnp.mean(x * x, axis=-1, keepdims=True)
    else:
        ms = jnp.sum(x * x, axis=-1, keepdims=True) * (1.0 / n)
    return (x * lax.rsqrt(ms + EPS)) * gain


def _log_sigmoid(x):
    return -(jnp.maximum(-x, 0.0) + jnp.log1p(jnp.exp(-jnp.abs(x))))


def _lane_tile(x, reps):
    return jnp.concatenate([x] * reps, axis=1)


def _dot(a, b):
    return jnp.dot(a, b, preferred_element_type=F32)


def _dot_nt(a, b):
    return lax.dot_general(a, b, (((1,), (1,)), ((), ())), preferred_element_type=F32)


def _dot_tn(a, b):
    return lax.dot_general(a, b, (((0,), (0,)), ((), ())), preferred_element_type=F32)


def _ffn_body(emit_next, cast_next, x_hbm, g_ref, wg_ref, wu_ref, wo_ref, *rest):
    rest = list(rest)
    ng_ref = rest.pop(0) if emit_next else None
    nwi_ref, nwo_ref = (rest.pop(0), rest.pop(0)) if cast_next else (None, None)
    o_ref = rest.pop(0)
    hn_ref = rest.pop(0) if emit_next else None
    nwi_out, nwo_out = (rest.pop(0), rest.pop(0)) if cast_next else (None, None)
    h_scr, x_buf, x_sem = rest
    i = pl.program_id(0)
    j = pl.program_id(1)

    if cast_next:
        nwi_out[...] = nwi_ref[...].astype(BF16)
        nwo_out[...] = nwo_ref[...].astype(BF16)

    def x_copy(tile):
        rows = pl.ds(pl.multiple_of(tile * TM_FFN, TM_FFN), TM_FFN)
        return pltpu.make_async_copy(x_hbm.at[rows, :], x_buf, x_sem)

    @pl.when(j == 0)
    def _():
        @pl.when(i == 0)
        def _():
            x_copy(0).start()

        x_copy(i).wait()
        x = x_buf[...]
        h_scr[...] = _rms(x, g_ref[...]).astype(BF16)
        o_ref[...] = x

    @pl.when(jnp.logical_and(j == 1, i + 1 < pl.num_programs(0)))
    def _():
        x_copy(i + 1).start()

    h = h_scr[...]
    g = _dot(h, wg_ref[...])
    u = _dot(h, wu_ref[...])
    a = ((g * jax.nn.sigmoid(g)) * u) * 0.5
    o_ref[...] += _dot(a.astype(BF16), wo_ref[...])

    if emit_next:
        @pl.when(j == pl.num_programs(1) - 1)
        def _():
            hn_ref[...] = _rms(o_ref[...], ng_ref[...]).astype(BF16)


def _ffn(x, gain, w_in, w_out, next_gain=None, next_weights=None):
    t, d = x.shape
    nf = D_FF // TF
    ni = t // TM_FFN
    assert nf >= 2 and t % TM_FFN == 0
    emit = next_gain is not None
    cast = next_weights is not None
    in_specs = [
        pl.BlockSpec(memory_space=pl.ANY),
        pl.BlockSpec((1, d), lambda i, j: (0, 0)),
        pl.BlockSpec((d, TF), lambda i, j: (0, j)),
        pl.BlockSpec((d, TF), lambda i, j: (0, j + nf)),
        pl.BlockSpec((TF, d), lambda i, j: (j, 0)),
    ]
    args = [x, gain.reshape(1, d), w_in, w_in, w_out]
    out_shape = [jax.ShapeDtypeStruct((t, d), F32)]
    out_specs = [pl.BlockSpec((TM_FFN, d), lambda i, j: (i, 0))]
    if emit:
        in_specs.append(pl.BlockSpec((1, d), lambda i, j: (0, 0)))
        args.append(next_gain.reshape(1, d))
        out_shape.append(jax.ShapeDtypeStruct((t, d), BF16))
        out_specs.append(pl.BlockSpec((TM_FFN, d), lambda i, j: (i, 0)))
    if cast:
        nwi, nwo, nl = next_weights
        bf16_rows = 2 * SUBLANES
        ri, ci = d // ni, 2 * D_FF // nf
        ro = D_FF // (ni * nf)
        assert d % ni == 0 and ri % bf16_rows == 0 and ci % LANES == 0
        assert D_FF % (ni * nf) == 0 and ro % bf16_rows == 0
        in_specs += [pl.BlockSpec((None, ri, ci), lambda i, j: (nl, i, j)),
                     pl.BlockSpec((None, ro, d), lambda i, j: (nl, i * nf + j, 0))]
        args += [nwi, nwo]
        out_shape += [jax.ShapeDtypeStruct((d, 2 * D_FF), BF16), jax.ShapeDtypeStruct((D_FF, d), BF16)]
        out_specs += [pl.BlockSpec((ri, ci), lambda i, j: (i, j)),
                      pl.BlockSpec((ro, d), lambda i, j: (i * nf + j, 0))]
    res = pl.pallas_call(
        functools.partial(_ffn_body, emit, cast),
        grid=(ni, nf),
        in_specs=in_specs,
        out_specs=out_specs,
        out_shape=out_shape,
        scratch_shapes=[pltpu.VMEM((TM_FFN, d), BF16),
                        pltpu.VMEM((TM_FFN, d), F32),
                        pltpu.SemaphoreType.DMA(())],
        compiler_params=_params(("arbitrary", "arbitrary"), VMEM_LIMIT_FFN),
        name="ffn" + ("_next" if emit else "") + ("_cast" if cast else ""),
    )(*args)
    x_new = res[0]
    hn = res[1] if emit else None
    next_bf16 = (res[-2], res[-1]) if cast else None
    return x_new, hn, next_bf16


def _resident(shape):
    return pl.BlockSpec(shape, lambda i: (0,) * len(shape), pipeline_mode=pl.Buffered(1))


def _proja_body(h_ref, wqk_ref, wv_ref, wo_ref, wmq_ref, wg_ref, wgt_ref, brow_ref, bcol_ref,
                qk_ref, v_ref, og_ref, mq_ref, g_ref, gt_ref):
    h = h_ref[...]
    qk_ref[...] = _dot(h, wqk_ref[...])
    v_ref[...] = _dot(h, wv_ref[...]).astype(v_ref.dtype)
    og_ref[...] = _dot(h, wo_ref[...])
    mq_ref[...] = _dot(h, wmq_ref[...])
    g = _dot(h, wg_ref[...]) + brow_ref[...]
    lane = lax.broadcasted_iota(jnp.int32, g.shape, 1)
    g_ref[...] = jnp.where(lane >= MLSTM_HEADS, _log_sigmoid(g), g)
    gt = _dot_nt(wgt_ref[...], h)[:2 * MLSTM_HEADS] + bcol_ref[...]
    row = lax.broadcasted_iota(jnp.int32, gt.shape, 0)
    gt_ref[...] = jnp.where(row >= MLSTM_HEADS, _log_sigmoid(gt), gt)


def _proj_a(h, w_qk, w_v, w_o, w_mq, w_g, w_gt, bias):
    t, d = h.shape
    ng = 2 * MLSTM_HEADS
    nqk = w_qk.shape[1]
    brow = jnp.zeros((1, GATE_PAD), F32).at[0, :ng].set(bias)
    bcol = bias.reshape(ng, 1)
    row_spec = lambda n: pl.BlockSpec((TM, n), lambda i: (i, 0))
    return pl.pallas_call(
        _proja_body,
        grid=(t // TM,),
        in_specs=[row_spec(d), _resident((d, nqk)), _resident((d, MAIN_WIDTH)), _resident((d, MAIN_WIDTH)),
                  _resident((d, MEM_WIDTH)), _resident((d, GATE_PAD)), _resident((2 * SUBLANES, d)),
                  _resident((1, GATE_PAD)), _resident((ng, 1))],
        out_specs=[row_spec(nqk), row_spec(MAIN_WIDTH), row_spec(MAIN_WIDTH), row_spec(MEM_WIDTH),
                   row_spec(GATE_PAD), pl.BlockSpec((ng, TM), lambda i: (0, i))],
        out_shape=[jax.ShapeDtypeStruct((t, nqk), F32),
                   jax.ShapeDtypeStruct((t, MAIN_WIDTH), BF16),
                   jax.ShapeDtypeStruct((t, MAIN_WIDTH), F32),
                   jax.ShapeDtypeStruct((t, MEM_WIDTH), F32),
                   jax.ShapeDtypeStruct((t, GATE_PAD), F32),
                   jax.ShapeDtypeStruct((ng, t), F32)],
        compiler_params=_params(("parallel",)),
        name="proj_mlstm",
    )(h, w_qk, w_v, w_o, w_mq, w_g, w_gt, brow, bcol)


def _mlstm_body(qk_ref, cw_ref, v_ref, og_ref, g_ref, gt_ref, hg_ref, out_ref, xbuf, c_scr, m_scr):
    nh, dkp, dv = MLSTM_HEADS, QK_PAD, MLSTM_V_DIM
    ell = CHUNK
    halo = SUBLANES

    @pl.when(pl.program_id(1) == 0)
    def _():
        xbuf[0:halo, :] = jnp.zeros((halo, xbuf.shape[1]), F32)
        c_scr[...] = jnp.zeros(c_scr.shape, F32)
        m_scr[...] = jnp.zeros(m_scr.shape, F32)

    xbuf[halo:halo + ell, :] = qk_ref[...]
    cw = cw_ref[...]
    y = cw[CONV_WIDTH - 1:CONV_WIDTH, :] * xbuf[halo:halo + ell, :]
    for j in range(CONV_WIDTH - 1):
        sh = CONV_WIDTH - 1 - j
        y = y + cw[j:j + 1, :] * xbuf[halo - sh:halo - sh + ell, :]
    xbuf[0:halo, :] = xbuf[ell:ell + halo, :]
    y = y * jax.nn.sigmoid(y)

    r_i = lax.broadcasted_iota(jnp.int32, (ell, ell), 0)
    c_i = lax.broadcasted_iota(jnp.int32, (ell, ell), 1)
    causal = c_i <= r_i
    tri = jnp.where(causal, 1.0, 0.0).astype(F32)
    tri_t = jnp.where(r_i <= c_i, 1.0, 0.0).astype(F32)
    gcol = g_ref[...]
    grow = gt_ref[...]
    bcol_all = jnp.dot(tri, gcol, preferred_element_type=F32, precision=lax.Precision.HIGHEST)
    brow_all = jnp.dot(grow, tri_t, preferred_element_type=F32, precision=lax.Precision.HIGHEST)

    lane = lax.broadcasted_iota(jnp.int32, (ell, LANES), 1)
    ones_col = jnp.where(lane == 0, 1.0, 0.0).astype(BF16)
    v_all = v_ref[...]
    og = og_ref[...]
    hg = hg_ref[...]

    for h in range(nh):
        q = (y[:, h * dkp:(h + 1) * dkp] * (MLSTM_QK_DIM ** -0.5)).astype(BF16)
        k_f = y[:, (nh + h) * dkp:(nh + h + 1) * dkp]
        k = k_f.astype(BF16)
        v_ext = jnp.concatenate([v_all[:, h * dv:(h + 1) * dv], ones_col], axis=-1)
        ig_col = gcol[:, h:h + 1]
        ig_row = grow[h:h + 1, :]
        b_col = bcol_all[:, nh + h:nh + h + 1]
        b_row = brow_all[nh + h:nh + h + 1, :]
        m_prev = m_scr[h][:, 0:1]
        c_prev = c_scr[h]

        log_w = jnp.where(causal, b_col - b_row + ig_row, -jnp.inf)
        log_inter = b_col + m_prev
        m_t = jnp.maximum(jnp.max(log_w, axis=-1, keepdims=True), log_inter)
        w = jnp.exp(log_w - m_t)
        w_inter = jnp.exp(log_inter - m_t)
        s = _dot_nt(q, k) * w
        inter = _dot(q, c_prev.astype(BF16))
        num = _dot(s.astype(BF16), v_ext[:, :dv]) + w_inter * inter[:, :dv]
        den = jnp.sum(s, axis=-1, keepdims=True) + w_inter * inter[:, dv:dv + 1]
        hh = num * (1.0 / jnp.maximum(jnp.abs(den), jnp.exp(-m_t)))

        g_last = b_col[ell - 1:ell, :]
        log_a = g_last - b_col + ig_col
        m_new = jnp.maximum(g_last + m_prev, jnp.max(log_a, axis=0, keepdims=True))
        decay = jnp.exp(g_last + m_prev - m_new)
        a = jnp.exp(log_a - m_new)
        c_scr[h] = decay * c_prev + _dot_tn((k_f * a).astype(BF16), v_ext)
        m_scr[h] = jnp.broadcast_to(m_new, (1, LANES))

        hn = _rms(hh, hg[:, h * dv:(h + 1) * dv])
        o_h = og[:, h * dv:(h + 1) * dv]
        out_ref[:, h * dv:(h + 1) * dv] = (jax.nn.sigmoid(o_h) * hn).astype(out_ref.dtype)


def _mlstm(qk, conv_w, v, og, g, gt, head_gain, bsz):
    t = qk.shape[0]
    nc = t // bsz // CHUNK
    nqk = qk.shape[1]
    return pl.pallas_call(
        _mlstm_body,
        grid=(bsz, nc),
        in_specs=[pl.BlockSpec((CHUNK, nqk), lambda b, c: (b * nc + c, 0)),
                  pl.BlockSpec((CONV_WIDTH, nqk), lambda b, c: (0, 0)),
                  pl.BlockSpec((CHUNK, MAIN_WIDTH), lambda b, c: (b * nc + c, 0)),
                  pl.BlockSpec((CHUNK, MAIN_WIDTH), lambda b, c: (b * nc + c, 0)),
                  pl.BlockSpec((CHUNK, GATE_PAD), lambda b, c: (b * nc + c, 0)),
                  pl.BlockSpec((2 * MLSTM_HEADS, CHUNK), lambda b, c: (0, b * nc + c)),
                  pl.BlockSpec((1, MAIN_WIDTH), lambda b, c: (0, 0))],
        out_specs=pl.BlockSpec((CHUNK, MAIN_WIDTH), lambda b, c: (b * nc + c, 0)),
        out_shape=jax.ShapeDtypeStruct((t, MAIN_WIDTH), BF16),
        scratch_shapes=[pltpu.VMEM((CHUNK + SUBLANES, nqk), F32),
                        pltpu.VMEM((MLSTM_HEADS, QK_PAD, MLSTM_V_DIM + LANES), F32),
                        pltpu.VMEM((MLSTM_HEADS, 1, LANES), F32)],
        compiler_params=_params(("parallel", "arbitrary")),
        name="mlstm",
    )(qk, conv_w, v, og, g, gt, head_gain.reshape(1, MAIN_WIDTH))


def _memkv_body(mem_ref, g_ref, w_ref, kg_ref, mk_ref, mv_ref):
    hn = _rms(mem_ref[...], g_ref[...]).astype(BF16)
    kv = _dot(hn, w_ref[...])
    kg = kg_ref[...]
    for h in range(MEM_HEADS):
        sl = slice(h * MEM_HEAD_DIM, (h + 1) * MEM_HEAD_DIM)
        mk_ref[:, sl] = _rms(kv[:, sl], kg).astype(mk_ref.dtype)
    mv_ref[...] = kv[:, MEM_WIDTH:].astype(mv_ref.dtype)


def _mem_kv(mem2d, mem_gain, w_mem_kv, mem_k_gain):
    rows = mem2d.shape[0]
    shp = jax.ShapeDtypeStruct((DEPTH, rows, MEM_WIDTH), BF16)
    return pl.pallas_call(
        _memkv_body,
        grid=(DEPTH,),
        in_specs=[pl.BlockSpec((rows, D_MODEL), lambda l: (0, 0)),
                  pl.BlockSpec((None, 1, D_MODEL), lambda l: (l, 0, 0)),
                  pl.BlockSpec((None, D_MODEL, 2 * MEM_WIDTH), lambda l: (l, 0, 0)),
                  pl.BlockSpec((None, 1, MEM_HEAD_DIM), lambda l: (l, 0, 0))],
        out_specs=[pl.BlockSpec((None, rows, MEM_WIDTH), lambda l: (l, 0, 0)),
                   pl.BlockSpec((None, rows, MEM_WIDTH), lambda l: (l, 0, 0))],
        out_shape=[shp, shp],
        compiler_params=_params(("parallel",)),
        name="mem_kv",
    )(mem2d, mem_gain.reshape(DEPTH, 1, D_MODEL), w_mem_kv, mem_k_gain.reshape(DEPTH, 1, MEM_HEAD_DIM))


def _mix_body(x_ref, main_ref, mq_ref, mk_ref, mv_ref, qg_ref, wmain_ref, wmem_ref, o_ref):
    mq = mq_ref[...]
    mk = mk_ref[...]
    mv = mv_ref[...]
    qg = qg_ref[...]
    heads = []
    for h in range(MEM_HEADS):
        sl = slice(h * MEM_HEAD_DIM, (h + 1) * MEM_HEAD_DIM)
        qn = _rms(mq[:, sl], qg).astype(BF16)
        s = _dot_nt(qn, mk[:, sl]) * (MEM_HEAD_DIM ** -0.5)
        e = jnp.exp(s - jnp.max(s, axis=-1, keepdims=True))
        oh = _dot(e.astype(BF16), mv[:, sl]) * (1.0 / jnp.sum(e, axis=-1, keepdims=True))
        heads.append(oh.astype(BF16))
    mem_out = jnp.concatenate(heads, axis=-1)
    o_ref[...] = x_ref[...] + _dot(main_ref[...], wmain_ref[...]) + _dot(mem_out, wmem_ref[...])


def _mixer_out(x, main, mq, mk, mv, q_gain, w_out, layer, seq):
    t, d = x.shape
    tiles_per_seq = seq // TM
    return pl.pallas_call(
        _mix_body,
        grid=(t // TM,),
        in_specs=[pl.BlockSpec((TM, d), lambda i: (i, 0)),
                  pl.BlockSpec((TM, MAIN_WIDTH), lambda i: (i, 0)),
                  pl.BlockSpec((TM, MEM_WIDTH), lambda i: (i, 0)),
                  pl.BlockSpec((None, N_MEM, MEM_WIDTH), lambda i: (layer, i // tiles_per_seq, 0)),
                  pl.BlockSpec((None, N_MEM, MEM_WIDTH), lambda i: (layer, i // tiles_per_seq, 0)),
                  pl.BlockSpec((None, 1, MEM_HEAD_DIM), lambda i: (layer, 0, 0)),
                  pl.BlockSpec((None, MAIN_WIDTH, d), lambda i: (layer, 0, 0)),
                  pl.BlockSpec((None, MEM_WIDTH, d), lambda i: (layer, MAIN_WIDTH // MEM_WIDTH, 0))],
        out_specs=pl.BlockSpec((TM, d), lambda i: (i, 0)),
        out_shape=jax.ShapeDtypeStruct((t, d), F32),
        compiler_params=_params(("parallel",)),
        name="mixer_out",
    )(x, main, mq, mk, mv, q_gain, w_out, w_out)


def _rope_body(pos_ref, freq_ref, c_ref, s_ref):
    ang = pos_ref[...].astype(F32) * freq_ref[...]
    lane = lax.broadcasted_iota(jnp.int32, ang.shape, 1)
    half = MLA_ROPE_DIM // 2
    cos = jnp.cos(ang)
    sin = jnp.sin(ang)
    c_ref[...] = jnp.where(lane < MLA_ROPE_DIM, cos, 0.0)
    s_ref[...] = jnp.where(lane < half, -sin, jnp.where(lane < MLA_ROPE_DIM, sin, 0.0))


def _rope_tables(positions):
    t = positions.size
    tr = min(TROPE, t)
    half = MLA_ROPE_DIM // 2
    inv_freq = ROPE_THETA ** (-jnp.arange(0, MLA_ROPE_DIM, 2, dtype=F32) / MLA_ROPE_DIM)
    freq = jnp.tile(inv_freq, LANES // half).reshape(1, LANES)
    shp = jax.ShapeDtypeStruct((t, LANES), F32)
    return pl.pallas_call(
        _rope_body,
        grid=(t // tr,),
        in_specs=[pl.BlockSpec((tr, 1), lambda i: (i, 0)),
                  pl.BlockSpec((1, LANES), lambda i: (0, 0))],
        out_specs=[pl.BlockSpec((tr, LANES), lambda i: (i, 0)),
                   pl.BlockSpec((tr, LANES), lambda i: (i, 0))],
        out_shape=[shp, shp],
        compiler_params=_params(("parallel",)),
        name="rope_tables",
    )(positions.reshape(t, 1), freq)


def _rope(u, cmul, smul):
    return u * cmul + (pltpu.roll(u, 96, 1) + pltpu.roll(u, 32, 1)) * smul


def _kv_body(h_ref, wd_ref, lg_ref, wu_ref, kgn_ref, kgr_ref, c_ref, s_ref, k_ref, v_ref):
    low = _dot(h_ref[...], wd_ref[...])
    ckv = _rms(low[:, :KV_LORA_RANK], lg_ref[...]).astype(BF16)
    kv = _dot(ckv, wu_ref[...])
    pe = low[:, KV_LORA_RANK:]
    pe_ss = jnp.sum(pe * pe, axis=-1, keepdims=True)
    pe_rot = _rope(pe * kgr_ref[...], c_ref[...], s_ref[...])
    kgn = kgn_ref[...]
    for h in range(MLA_HEADS):
        kn = kv[:, h * HEAD_PAD:h * HEAD_PAD + MLA_NOPE_DIM]
        ms = (jnp.sum(kn * kn, axis=-1, keepdims=True) + pe_ss) * (1.0 / MLA_QK_DIM)
        rinv = lax.rsqrt(ms + EPS)
        k_ref[0, h, :, 0:MLA_NOPE_DIM] = ((kn * rinv) * kgn).astype(k_ref.dtype)
        k_ref[0, h, :, MLA_NOPE_DIM:HEAD_PAD] = (pe_rot * rinv).astype(k_ref.dtype)
        v_ref[0, h, :, 0:MLA_V_DIM] = kv[:, h * HEAD_PAD + MLA_NOPE_DIM:(h + 1) * HEAD_PAD].astype(v_ref.dtype)
        v_ref[0, h, :, MLA_V_DIM:V_EXT] = jnp.ones((kv.shape[0], V_EXT - MLA_V_DIM), v_ref.dtype)


def _shared_kv(h, w_dkv_pad, latent_gain, w_ukv, kg_nope, kg_rope_pad, cmul, smul, bsz):
    t, d = h.shape
    seq = t // bsz
    tps = seq // TM
    return pl.pallas_call(
        _kv_body,
        grid=(t // TM,),
        in_specs=[pl.BlockSpec((TM, d), lambda i: (i, 0)),
                  pl.BlockSpec((d, KV_DOWN_PAD), lambda i: (0, 0)),
                  pl.BlockSpec((1, KV_LORA_RANK), lambda i: (0, 0)),
                  pl.BlockSpec((KV_LORA_RANK, MLA_HEADS * HEAD_PAD), lambda i: (0, 0)),
                  pl.BlockSpec((1, MLA_NOPE_DIM), lambda i: (0, 0)),
                  pl.BlockSpec((1, LANES), lambda i: (0, 0)),
                  pl.BlockSpec((TM, LANES), lambda i: (i, 0)),
                  pl.BlockSpec((TM, LANES), lambda i: (i, 0))],
        out_specs=[pl.BlockSpec((1, MLA_HEADS, TM, HEAD_PAD), lambda i: (i // tps, 0, i % tps, 0)),
                   pl.BlockSpec((1, MLA_HEADS, TM, V_EXT), lambda i: (i // tps, 0, i % tps, 0))],
        out_shape=[jax.ShapeDtypeStruct((bsz, MLA_HEADS, seq, HEAD_PAD), BF16),
                   jax.ShapeDtypeStruct((bsz, MLA_HEADS, seq, V_EXT), BF16)],
        compiler_params=_params(("parallel",)),
        name="shared_kv",
    )(h, w_dkv_pad, latent_gain, w_ukv, kg_nope, kg_rope_pad, cmul, smul)


def _q_body(h_ref, wb_ref, lg_ref, wu_ref, qg_ref, c_ref, s_ref, mq_ref, q_ref):
    low = _dot(h_ref[...], wb_ref[...])
    mq_ref[...] = low[:, Q_LORA_PAD:]
    cq = _rms(low[:, :Q_LORA_PAD], lg_ref[...], n=Q_LORA_RANK).astype(BF16)
    q = _dot(cq, wu_ref[...])
    qg = qg_ref[...]
    cmul = c_ref[...]
    smul = s_ref[...]
    c = (MLA_QK_DIM ** -0.5) * LOG2_E
    for h in range(MLA_HEADS):
        qh = _rms(q[:, h * HEAD_PAD:(h + 1) * HEAD_PAD], qg, n=MLA_QK_DIM)
        q_ref[0, h, :, 0:MLA_NOPE_DIM] = (qh[:, :MLA_NOPE_DIM] * c).astype(q_ref.dtype)
        q_ref[0, h, :, MLA_NOPE_DIM:HEAD_PAD] = (_rope(qh[:, MLA_NOPE_DIM:], cmul, smul) * c).astype(q_ref.dtype)


def _mla_q(h, w_b_pad, latent_gain_pad, w_uq_pad, q_gain_pad, cmul, smul, bsz):
    t, d = h.shape
    seq = t // bsz
    tps = seq // TM
    return pl.pallas_call(
        _q_body,
        grid=(t // TM,),
        in_specs=[pl.BlockSpec((TM, d), lambda i: (i, 0)),
                  _resident((d, Q_LORA_PAD + MEM_WIDTH)),
                  _resident((1, Q_LORA_PAD)),
                  _resident((Q_LORA_PAD, MLA_HEADS * HEAD_PAD)),
                  _resident((1, HEAD_PAD)),
                  pl.BlockSpec((TM, LANES), lambda i: (i, 0)),
                  pl.BlockSpec((TM, LANES), lambda i: (i, 0))],
        out_specs=[pl.BlockSpec((TM, MEM_WIDTH), lambda i: (i, 0)),
                   pl.BlockSpec((1, MLA_HEADS, TM, HEAD_PAD), lambda i: (i // tps, 0, i % tps, 0))],
        out_shape=[jax.ShapeDtypeStruct((t, MEM_WIDTH), F32),
                   jax.ShapeDtypeStruct((bsz, MLA_HEADS, seq, HEAD_PAD), BF16)],
        compiler_params=_params(("parallel",)),
        name="proj_mla_q",
    )(h, w_b_pad, latent_gain_pad, w_uq_pad, q_gain_pad, cmul, smul)


def _flash_body(qi_tab, ki_tab, last_tab, q_ref, k_ref, v_ref, o_ref, m_scr, acc_scr):
    p_id = pl.program_id(1)
    diag = qi_tab[p_id] == ki_tab[p_id]
    reps = TQ // LANES

    def scores(h):
        return _dot_nt(q_ref[0, h], k_ref[0, h])

    @pl.when(diag)
    def _():
        r_i = lax.broadcasted_iota(jnp.int32, (TQ, TQ), 0)
        c_i = lax.broadcasted_iota(jnp.int32, (TQ, TQ), 1)
        keep = c_i <= r_i
        for h in range(MLA_HEADS):
            s = jnp.where(keep, scores(h), -jnp.inf)
            m = jnp.broadcast_to(jnp.max(s, axis=-1, keepdims=True), (TQ, LANES))
            e = jnp.exp2(s - _lane_tile(m, reps))
            m_scr[h] = m
            acc_scr[h] = _dot(e.astype(BF16), v_ref[0, h])

    @pl.when(jnp.logical_not(diag))
    def _():
        for h in range(MLA_HEADS):
            s = scores(h)
            m_prev = m_scr[h]
            m = jnp.maximum(m_prev, jnp.max(s, axis=-1, keepdims=True))
            alpha = jnp.exp2(m_prev - m)
            e = jnp.exp2(s - _lane_tile(m, reps))
            m_scr[h] = m
            acc_scr[h] = _lane_tile(alpha, V_EXT // LANES) * acc_scr[h] + _dot(e.astype(BF16), v_ref[0, h])

    @pl.when(last_tab[p_id] == 1)
    def _():
        for h in range(MLA_HEADS):
            acc = acc_scr[h]
            o_ref[:, h * MLA_V_DIM:(h + 1) * MLA_V_DIM] = (
                acc[:, :MLA_V_DIM] * (1.0 / acc[:, MLA_V_DIM:])).astype(o_ref.dtype)


def _flash(q, k, v):
    bsz, nh, seq, _ = q.shape
    nq = seq // TQ
    qi, ki, last = [], [], []
    for a in range(nq):
        order = [a] + list(range(a))
        for n, b in enumerate(order):
            qi.append(a)
            ki.append(b)
            last.append(1 if n == len(order) - 1 else 0)
    npairs = len(qi)
    tabs = [jnp.asarray(x, jnp.int32) for x in (qi, ki, last)]
    grid_spec = pltpu.PrefetchScalarGridSpec(
        num_scalar_prefetch=3,
        grid=(bsz, npairs),
        in_specs=[pl.BlockSpec((1, nh, TQ, HEAD_PAD), lambda b, p, qt, kt, lt: (b, 0, qt[p], 0)),
                  pl.BlockSpec((1, nh, TQ, HEAD_PAD), lambda b, p, qt, kt, lt: (b, 0, kt[p], 0)),
                  pl.BlockSpec((1, nh, TQ, V_EXT), lambda b, p, qt, kt, lt: (b, 0, kt[p], 0))],
        out_specs=pl.BlockSpec((TQ, nh * MLA_V_DIM), lambda b, p, qt, kt, lt: (b * nq + qt[p], 0)),
        scratch_shapes=[pltpu.VMEM((nh, TQ, LANES), F32),
                        pltpu.VMEM((nh, TQ, V_EXT), F32)],
    )
    return pl.pallas_call(
        _flash_body,
        grid_spec=grid_spec,
        out_shape=jax.ShapeDtypeStruct((bsz * seq, nh * MLA_V_DIM), BF16),
        compiler_params=_params(("parallel", "arbitrary")),
        name="mla_attention",
    )(*tabs, q, k, v)


def _pad_heads(w, nh, dh, dh_pad):
    lead = w.shape[:-1]
    w = w.reshape(lead + (nh, dh))
    w = jnp.pad(w, [(0, 0)] * len(lead) + [(0, 0), (0, dh_pad - dh)])
    return w.reshape(lead + (nh * dh_pad,))


def _pad_last(w, n):
    return jnp.pad(w, [(0, 0)] * (w.ndim - 1) + [(0, n - w.shape[-1])])


def kernel(x, mem, positions, ffn1_gain, ffn1_w_in, ffn1_w_out, mix_gain, w_out, mem_gain, w_mem_kv, mem_q_gain, mem_k_gain, a_w_in, a_b_gates, a_conv, a_head_gain, kv_gain, w_dkv, kv_latent_gain, w_ukv, k_gain, b_w_in, b_q_latent_gain, b_w_uq, b_q_gain, ffn2_gain, ffn2_w_in, ffn2_w_out):
    bsz, seq, d = x.shape
    t = bsz * seq
    assert d == D_MODEL and seq % TM == 0 and seq % CHUNK == 0 and seq % TQ == 0 and t % min(TROPE, t) == 0 and t % TM_FFN == 0

    ffn_w = (ffn1_w_in[0].astype(BF16), ffn1_w_out[0].astype(BF16))
    w_out_b = w_out.astype(BF16)

    nqk = MLSTM_HEADS * MLSTM_QK_DIM
    c0, c1, c2, c3 = 2 * nqk, 2 * nqk + MAIN_WIDTH, 2 * nqk + 2 * MAIN_WIDTH, 2 * nqk + 2 * MAIN_WIDTH + 2 * MLSTM_HEADS

    cmul, smul = _rope_tables(positions)
    mk, mv = _mem_kv(mem.reshape(bsz * N_MEM, d), mem_gain, w_mem_kv.astype(BF16), mem_k_gain)

    xs = x.reshape(t, d)
    k_sh = v_sh = None
    for layer in range(DEPTH):
        xs, hmix, ffn_w = _ffn(xs, ffn1_gain[layer], *ffn_w, next_gain=mix_gain[layer],
                               next_weights=(ffn2_w_in, ffn2_w_out, layer))
        if layer < N_A_LAYERS:
            w_in = a_w_in[layer]
            w_qk = jnp.concatenate([_pad_heads(w_in[:, :nqk], MLSTM_HEADS, MLSTM_QK_DIM, QK_PAD),
                                    _pad_heads(w_in[:, nqk:c0], MLSTM_HEADS, MLSTM_QK_DIM, QK_PAD)], axis=-1)
            conv_w = jnp.concatenate([_pad_heads(a_conv[layer][:, :nqk], MLSTM_HEADS, MLSTM_QK_DIM, QK_PAD),
                                      _pad_heads(a_conv[layer][:, nqk:], MLSTM_HEADS, MLSTM_QK_DIM, QK_PAD)], axis=-1)
            w_g = w_in[:, c2:c3]
            qk, v, og, mq, g, gt = _proj_a(
                hmix, w_qk.astype(BF16), w_in[:, c0:c1].astype(BF16), w_in[:, c1:c2].astype(BF16),
                w_in[:, c3:].astype(BF16), _pad_last(w_g, GATE_PAD).astype(BF16),
                jnp.pad(w_g.T, ((0, 2 * SUBLANES - 2 * MLSTM_HEADS), (0, 0))).astype(BF16), a_b_gates[layer])
            main = _mlstm(qk, conv_w, v, og, g, gt, a_head_gain[layer], bsz)
        else:
            j = layer - N_A_LAYERS
            w_in = b_w_in[j]
            w_b = jnp.concatenate([_pad_last(w_in[:, :Q_LORA_RANK], Q_LORA_PAD), w_in[:, Q_LORA_RANK:]], axis=-1)
            w_uq = jnp.pad(_pad_heads(b_w_uq[j], MLA_HEADS, MLA_QK_DIM, HEAD_PAD),
                           ((0, Q_LORA_PAD - Q_LORA_RANK), (0, 0))).astype(BF16)
            mq, q = _mla_q(hmix, w_b.astype(BF16), _pad_last(b_q_latent_gain[j], Q_LORA_PAD).reshape(1, Q_LORA_PAD),
                           w_uq, _pad_last(b_q_gain[j], HEAD_PAD).reshape(1, HEAD_PAD), cmul, smul, bsz)
            main = _flash(q, k_sh, v_sh)
        xs = _mixer_out(xs, main, mq, mk, mv, mem_q_gain.reshape(DEPTH, 1, MEM_HEAD_DIM), w_out_b, layer, seq)
        following = (ffn1_w_in, ffn1_w_out, layer + 1) if layer + 1 < DEPTH else None
        xs, hkv, ffn_w = _ffn(xs, ffn2_gain[layer], *ffn_w, next_gain=kv_gain if layer == N_A_LAYERS - 1 else None,
                              next_weights=following)
        if layer == N_A_LAYERS - 1:
            k_sh, v_sh = _shared_kv(hkv, _pad_last(w_dkv, KV_DOWN_PAD).astype(BF16),
                                    kv_latent_gain.reshape(1, KV_LORA_RANK), w_ukv.astype(BF16),
                                    k_gain[:MLA_NOPE_DIM].reshape(1, MLA_NOPE_DIM),
                                    _pad_last(k_gain[MLA_NOPE_DIM:], LANES).reshape(1, LANES), cmul, smul, bsz)
    return xs.reshape(bsz, seq, d)
```

```python
import functools

import jax
import jax.numpy as jnp
from jax import lax
from jax.experimental import pallas as pl
from jax.experimental.pallas import tpu as pltpu

F32 = jnp.float32
BF16 = jnp.bfloat16

D_MODEL = 2048
DEPTH = 4
N_MEM = 256
N_A_LAYERS = DEPTH // 2
EPS = 1e-6
MEM_HEADS = 4
MEM_WIDTH = D_MODEL // 4
MEM_HEAD_DIM = MEM_WIDTH // MEM_HEADS
MAIN_WIDTH = D_MODEL - MEM_WIDTH
MLSTM_HEADS = 4
MLSTM_V_DIM = MAIN_WIDTH // MLSTM_HEADS
MLSTM_QK_DIM = MLSTM_V_DIM // 2
CONV_WIDTH = 4
MLA_NOPE_DIM = 128
MLA_ROPE_DIM = 64
MLA_QK_DIM = MLA_NOPE_DIM + MLA_ROPE_DIM
MLA_V_DIM = 128
MLA_HEADS = MAIN_WIDTH // MLA_V_DIM
Q_LORA_RANK = 448
KV_LORA_RANK = 512
ROPE_THETA = 10000.0
D_FF = 5632
LOG2_E = 1.4426950408889634

LANES = 128
SUBLANES = 8
V7X_VMEM_BYTES = 64 * 1024 * 1024

QK_PAD = 256
Q_LORA_PAD = 512
HEAD_PAD = 256
GATE_PAD = LANES
KV_DOWN_PAD = KV_LORA_RANK + LANES
V_EXT = MLA_V_DIM + LANES

TM = 512
TM_FFN = 1024
TF = 512
CHUNK = 256
TQ = 512
TROPE = 2048
MIB = 1024 * 1024
VMEM_LIMIT = 56 * MIB
VMEM_LIMIT_FFN = 58 * MIB
assert VMEM_LIMIT_FFN < V7X_VMEM_BYTES


def _params(sem, vmem_limit=VMEM_LIMIT):
    return pltpu.CompilerParams(dimension_semantics=sem, vmem_limit_bytes=vmem_limit)


def _rms(x, gain, n=None):
    if n is None:
        ms = jnp.mean(x * x, axis=-1, keepdims=True)
    else:
        ms = jnp.sum(x * x, axis=-1, keepdims=True) * (1.0 / n)
    return (x * lax.rsqrt(ms + EPS)) * gain


def _log_sigmoid(x):
    return -(jnp.maximum(-x, 0.0) + jnp.log1p(jnp.exp(-jnp.abs(x))))


def _lane_tile(x, reps):
    return jnp.concatenate([x] * reps, axis=1)


def _dot(a, b):
    return jnp.dot(a, b, preferred_element_type=F32)


def _dot_nt(a, b):
    return lax.dot_general(a, b, (((1,), (1,)), ((), ())), preferred_element_type=F32)


def _dot_tn(a, b):
    return lax.dot_general(a, b, (((0,), (0,)), ((), ())), preferred_element_type=F32)


def _ffn_body(emit_next, cast_next, x_hbm, g_ref, wg_ref, wu_ref, wo_ref, *rest):
    rest = list(rest)
    ng_ref = rest.pop(0) if emit_next else None
    nwi_ref, nwo_ref = (rest.pop(0), rest.pop(0)) if cast_next else (None, None)
    o_ref = rest.pop(0)
    hn_ref = rest.pop(0) if emit_next else None
    nwi_out, nwo_out = (rest.pop(0), rest.pop(0)) if cast_next else (None, None)
    h_scr, x_buf, x_sem = rest
    i = pl.program_id(0)
    j = pl.program_id(1)

    if cast_next:
        nwi_out[...] = nwi_ref[...].astype(BF16)
        nwo_out[...] = nwo_ref[...].astype(BF16)

    def x_copy(tile):
        rows = pl.ds(pl.multiple_of(tile * TM_FFN, TM_FFN), TM_FFN)
        return pltpu.make_async_copy(x_hbm.at[rows, :], x_buf, x_sem)

    @pl.when(j == 0)
    def _():
        @pl.when(i == 0)
        def _():
            x_copy(0).start()

        x_copy(i).wait()
        x = x_buf[...]
        h_scr[...] = _rms(x, g_ref[...]).astype(BF16)
        o_ref[...] = x

    @pl.when(jnp.logical_and(j == 1, i + 1 < pl.num_programs(0)))
    def _():
        x_copy(i + 1).start()

    h = h_scr[...]
    g = _dot(h, wg_ref[...])
    u = _dot(h, wu_ref[...])
    a = ((g * jax.nn.sigmoid(g)) * u) * 0.5
    o_ref[...] += _dot(a.astype(BF16), wo_ref[...])

    if emit_next:
        @pl.when(j == pl.num_programs(1) - 1)
        def _():
            hn_ref[...] = _rms(o_ref[...], ng_ref[...]).astype(BF16)


def _ffn(x, gain, w_in, w_out, next_gain=None, next_weights=None):
    t, d = x.shape
    nf = D_FF // TF
    ni = t // TM_FFN
    assert nf >= 2 and t % TM_FFN == 0
    emit = next_gain is not None
    cast = next_weights is not None
    in_specs = [
        pl.BlockSpec(memory_space=pl.ANY),
        pl.BlockSpec((1, d), lambda i, j: (0, 0)),
        pl.BlockSpec((d, TF), lambda i, j: (0, j)),
        pl.BlockSpec((d, TF), lambda i, j: (0, j + nf)),
        pl.BlockSpec((TF, d), lambda i, j: (j, 0)),
    ]
    args = [x, gain.reshape(1, d), w_in, w_in, w_out]
    out_shape = [jax.ShapeDtypeStruct((t, d), F32)]
    out_specs = [pl.BlockSpec((TM_FFN, d), lambda i, j: (i, 0))]
    if emit:
        in_specs.append(pl.BlockSpec((1, d), lambda i, j: (0, 0)))
        args.append(next_gain.reshape(1, d))
        out_shape.append(jax.ShapeDtypeStruct((t, d), BF16))
        out_specs.append(pl.BlockSpec((TM_FFN, d), lambda i, j: (i, 0)))
    if cast:
        nwi, nwo, nl = next_weights
        bf16_rows = 2 * SUBLANES
        ri, ci = d // ni, 2 * D_FF // nf
        ro = D_FF // (ni * nf)
        assert d % ni == 0 and ri % bf16_rows == 0 and ci % LANES == 0
        assert D_FF % (ni * nf) == 0 and ro % bf16_rows == 0
        in_specs += [pl.BlockSpec((None, ri, ci), lambda i, j: (nl, i, j)),
                     pl.BlockSpec((None, ro, d), lambda i, j: (nl, i * nf + j, 0))]
        args += [nwi, nwo]
        out_shape += [jax.ShapeDtypeStruct((d, 2 * D_FF), BF16), jax.ShapeDtypeStruct((D_FF, d), BF16)]
        out_specs += [pl.BlockSpec((ri, ci), lambda i, j: (i, j)),
                      pl.BlockSpec((ro, d), lambda i, j: (i * nf + j, 0))]
    res = pl.pallas_call(
        functools.partial(_ffn_body, emit, cast),
        grid=(ni, nf),
        in_specs=in_specs,
        out_specs=out_specs,
        out_shape=out_shape,
        scratch_shapes=[pltpu.VMEM((TM_FFN, d), BF16),
                        pltpu.VMEM((TM_FFN, d), F32),
                        pltpu.SemaphoreType.DMA(())],
        compiler_params=_params(("arbitrary", "arbitrary"), VMEM_LIMIT_FFN),
        name="ffn" + ("_next" if emit else "") + ("_cast" if cast else ""),
    )(*args)
    x_new = res[0]
    hn = res[1] if emit else None
    next_bf16 = (res[-2], res[-1]) if cast else None
    return x_new, hn, next_bf16


def _resident(shape):
    return pl.BlockSpec(shape, lambda i: (0,) * len(shape), pipeline_mode=pl.Buffered(1))


def _proja_body(h_ref, wqk_ref, wv_ref, wo_ref, wmq_ref, wg_ref, wgt_ref, brow_ref, bcol_ref,
                qk_ref, v_ref, og_ref, mq_ref, g_ref, gt_ref):
    h = h_ref[...]
    qk_ref[...] = _dot(h, wqk_ref[...])
    v_ref[...] = _dot(h, wv_ref[...]).astype(v_ref.dtype)
    og_ref[...] = _dot(h, wo_ref[...])
    mq_ref[...] = _dot(h, wmq_ref[...])
    g = _dot(h, wg_ref[...]) + brow_ref[...]
    lane = lax.broadcasted_iota(jnp.int32, g.shape, 1)
    g_ref[...] = jnp.where(lane >= MLSTM_HEADS, _log_sigmoid(g), g)
    gt = _dot_nt(wgt_ref[...], h)[:2 * MLSTM_HEADS] + bcol_ref[...]
    row = lax.broadcasted_iota(jnp.int32, gt.shape, 0)
    gt_ref[...] = jnp.where(row >= MLSTM_HEADS, _log_sigmoid(gt), gt)


def _proj_a(h, w_qk, w_v, w_o, w_mq, w_g, w_gt, bias):
    t, d = h.shape
    ng = 2 * MLSTM_HEADS
    nqk = w_qk.shape[1]
    brow = jnp.zeros((1, GATE_PAD), F32).at[0, :ng].set(bias)
    bcol = bias.reshape(ng, 1)
    row_spec = lambda n: pl.BlockSpec((TM, n), lambda i: (i, 0))
    return pl.pallas_call(
        _proja_body,
        grid=(t // TM,),
        in_specs=[row_spec(d), _resident((d, nqk)), _resident((d, MAIN_WIDTH)), _resident((d, MAIN_WIDTH)),
                  _resident((d, MEM_WIDTH)), _resident((d, GATE_PAD)), _resident((2 * SUBLANES, d)),
                  _resident((1, GATE_PAD)), _resident((ng, 1))],
        out_specs=[row_spec(nqk), row_spec(MAIN_WIDTH), row_spec(MAIN_WIDTH), row_spec(MEM_WIDTH),
                   row_spec(GATE_PAD), pl.BlockSpec((ng, TM), lambda i: (0, i))],
        out_shape=[jax.ShapeDtypeStruct((t, nqk), F32),
                   jax.ShapeDtypeStruct((t, MAIN_WIDTH), BF16),
                   jax.ShapeDtypeStruct((t, MAIN_WIDTH), F32),
                   jax.ShapeDtypeStruct((t, MEM_WIDTH), F32),
                   jax.ShapeDtypeStruct((t, GATE_PAD), F32),
                   jax.ShapeDtypeStruct((ng, t), F32)],
        compiler_params=_params(("parallel",)),
        name="proj_mlstm",
    )(h, w_qk, w_v, w_o, w_mq, w_g, w_gt, brow, bcol)


def _mlstm_body(qk_ref, cw_ref, v_ref, og_ref, g_ref, gt_ref, hg_ref, out_ref, xbuf, c_scr, m_scr):
    nh, dkp, dv = MLSTM_HEADS, QK_PAD, MLSTM_V_DIM
    ell = CHUNK
    halo = SUBLANES

    @pl.when(pl.program_id(1) == 0)
    def _():
        xbuf[0:halo, :] = jnp.zeros((halo, xbuf.shape[1]), F32)
        c_scr[...] = jnp.zeros(c_scr.shape, F32)
        m_scr[...] = jnp.zeros(m_scr.shape, F32)

    xbuf[halo:halo + ell, :] = qk_ref[...]
    cw = cw_ref[...]
    y = cw[CONV_WIDTH - 1:CONV_WIDTH, :] * xbuf[halo:halo + ell, :]
    for j in range(CONV_WIDTH - 1):
        sh = CONV_WIDTH - 1 - j
        y = y + cw[j:j + 1, :] * xbuf[halo - sh:halo - sh + ell, :]
    xbuf[0:halo, :] = xbuf[ell:ell + halo, :]
    y = y * jax.nn.sigmoid(y)

    r_i = lax.broadcasted_iota(jnp.int32, (ell, ell), 0)
    c_i = lax.broadcasted_iota(jnp.int32, (ell, ell), 1)
    causal = c_i <= r_i
    tri = jnp.where(causal, 1.0, 0.0).astype(F32)
    tri_t = jnp.where(r_i <= c_i, 1.0, 0.0).astype(F32)
    gcol = g_ref[...]
    grow = gt_ref[...]
    bcol_all = jnp.dot(tri, gcol, preferred_element_type=F32, precision=lax.Precision.HIGHEST)
    brow_all = jnp.dot(grow, tri_t, preferred_element_type=F32, precision=lax.Precision.HIGHEST)

    lane = lax.broadcasted_iota(jnp.int32, (ell, LANES), 1)
    ones_col = jnp.where(lane == 0, 1.0, 0.0).astype(BF16)
    v_all = v_ref[...]
    og = og_ref[...]
    hg = hg_ref[...]

    for h in range(nh):
        q = (y[:, h * dkp:(h + 1) * dkp] * (MLSTM_QK_DIM ** -0.5)).astype(BF16)
        k_f = y[:, (nh + h) * dkp:(nh + h + 1) * dkp]
        k = k_f.astype(BF16)
        v_ext = jnp.concatenate([v_all[:, h * dv:(h + 1) * dv], ones_col], axis=-1)
        ig_col = gcol[:, h:h + 1]
        ig_row = grow[h:h + 1, :]
        b_col = bcol_all[:, nh + h:nh + h + 1]
        b_row = brow_all[nh + h:nh + h + 1, :]
        m_prev = m_scr[h][:, 0:1]
        c_prev = c_scr[h]

        log_w = jnp.where(causal, b_col - b_row + ig_row, -jnp.inf)
        log_inter = b_col + m_prev
        m_t = jnp.maximum(jnp.max(log_w, axis=-1, keepdims=True), log_inter)
        w = jnp.exp(log_w - m_t)
        w_inter = jnp.exp(log_inter - m_t)
        s = _dot_nt(q, k) * w
        inter = _dot(q, c_prev.astype(BF16))
        num = _dot(s.astype(BF16), v_ext[:, :dv]) + w_inter * inter[:, :dv]
        den = jnp.sum(s, axis=-1, keepdims=True) + w_inter * inter[:, dv:dv + 1]
        hh = num * (1.0 / jnp.maximum(jnp.abs(den), jnp.exp(-m_t)))

        g_last = b_col[ell - 1:ell, :]
        log_a = g_last - b_col + ig_col
        m_new = jnp.maximum(g_last + m_prev, jnp.max(log_a, axis=0, keepdims=True))
        decay = jnp.exp(g_last + m_prev - m_new)
        a = jnp.exp(log_a - m_new)
        c_scr[h] = decay * c_prev + _dot_tn((k_f * a).astype(BF16), v_ext)
        m_scr[h] = jnp.broadcast_to(m_new, (1, LANES))

        hn = _rms(hh, hg[:, h * dv:(h + 1) * dv])
        o_h = og[:, h * dv:(h + 1) * dv]
        out_ref[:, h * dv:(h + 1) * dv] = (jax.nn.sigmoid(o_h) * hn).astype(out_ref.dtype)


def _mlstm(qk, conv_w, v, og, g, gt, head_gain, bsz):
    t = qk.shape[0]
    nc = t // bsz // CHUNK
    nqk = qk.shape[1]
    return pl.pallas_call(
        _mlstm_body,
        grid=(bsz, nc),
        in_specs=[pl.BlockSpec((CHUNK, nqk), lambda b, c: (b * nc + c, 0)),
                  pl.BlockSpec((CONV_WIDTH, nqk), lambda b, c: (0, 0)),
                  pl.BlockSpec((CHUNK, MAIN_WIDTH), lambda b, c: (b * nc + c, 0)),
                  pl.BlockSpec((CHUNK, MAIN_WIDTH), lambda b, c: (b * nc + c, 0)),
                  pl.BlockSpec((CHUNK, GATE_PAD), lambda b, c: (b * nc + c, 0)),
                  pl.BlockSpec((2 * MLSTM_HEADS, CHUNK), lambda b, c: (0, b * nc + c)),
                  pl.BlockSpec((1, MAIN_WIDTH), lambda b, c: (0, 0))],
        out_specs=pl.BlockSpec((CHUNK, MAIN_WIDTH), lambda b, c: (b * nc + c, 0)),
        out_shape=jax.ShapeDtypeStruct((t, MAIN_WIDTH), BF16),
        scratch_shapes=[pltpu.VMEM((CHUNK + SUBLANES, nqk), F32),
                        pltpu.VMEM((MLSTM_HEADS, QK_PAD, MLSTM_V_DIM + LANES), F32),
                        pltpu.VMEM((MLSTM_HEADS, 1, LANES), F32)],
        compiler_params=_params(("parallel", "arbitrary")),
        name="mlstm",
    )(qk, conv_w, v, og, g, gt, head_gain.reshape(1, MAIN_WIDTH))


def _memkv_body(mem_ref, g_ref, w_ref, kg_ref, mk_ref, mv_ref):
    hn = _rms(mem_ref[...], g_ref[...]).astype(BF16)
    kv = _dot(hn, w_ref[...])
    kg = kg_ref[...]
    for h in range(MEM_HEADS):
        sl = slice(h * MEM_HEAD_DIM, (h + 1) * MEM_HEAD_DIM)
        mk_ref[:, sl] = _rms(kv[:, sl], kg).astype(mk_ref.dtype)
    mv_ref[...] = kv[:, MEM_WIDTH:].astype(mv_ref.dtype)


def _mem_kv(mem2d, mem_gain, w_mem_kv, mem_k_gain):
    rows = mem2d.shape[0]
    shp = jax.ShapeDtypeStruct((DEPTH, rows, MEM_WIDTH), BF16)
    return pl.pallas_call(
        _memkv_body,
        grid=(DEPTH,),
        in_specs=[pl.BlockSpec((rows, D_MODEL), lambda l: (0, 0)),
                  pl.BlockSpec((None, 1, D_MODEL), lambda l: (l, 0, 0)),
                  pl.BlockSpec((None, D_MODEL, 2 * MEM_WIDTH), lambda l: (l, 0, 0)),
                  pl.BlockSpec((None, 1, MEM_HEAD_DIM), lambda l: (l, 0, 0))],
        out_specs=[pl.BlockSpec((None, rows, MEM_WIDTH), lambda l: (l, 0, 0)),
                   pl.BlockSpec((None, rows, MEM_WIDTH), lambda l: (l, 0, 0))],
        out_shape=[shp, shp],
        compiler_params=_params(("parallel",)),
        name="mem_kv",
    )(mem2d, mem_gain.reshape(DEPTH, 1, D_MODEL), w_mem_kv, mem_k_gain.reshape(DEPTH, 1, MEM_HEAD_DIM))


def _mix_body(x_ref, main_ref, mq_ref, mk_ref, mv_ref, qg_ref, wmain_ref, wmem_ref, o_ref):
    mq = mq_ref[...]
    mk = mk_ref[...]
    mv = mv_ref[...]
    qg = qg_ref[...]
    heads = []
    for h in range(MEM_HEADS):
        sl = slice(h * MEM_HEAD_DIM, (h + 1) * MEM_HEAD_DIM)
        qn = _rms(mq[:, sl], qg).astype(BF16)
        s = _dot_nt(qn, mk[:, sl]) * (MEM_HEAD_DIM ** -0.5)
        e = jnp.exp(s - jnp.max(s, axis=-1, keepdims=True))
        oh = _dot(e.astype(BF16), mv[:, sl]) * (1.0 / jnp.sum(e, axis=-1, keepdims=True))
        heads.append(oh.astype(BF16))
    mem_out = jnp.concatenate(heads, axis=-1)
    o_ref[...] = x_ref[...] + _dot(main_ref[...], wmain_ref[...]) + _dot(mem_out, wmem_ref[...])


def _mixer_out(x, main, mq, mk, mv, q_gain, w_out, layer, seq):
    t, d = x.shape
    tiles_per_seq = seq // TM
    return pl.pallas_call(
        _mix_body,
        grid=(t // TM,),
        in_specs=[pl.BlockSpec((TM, d), lambda i: (i, 0)),
                  pl.BlockSpec((TM, MAIN_WIDTH), lambda i: (i, 0)),
                  pl.BlockSpec((TM, MEM_WIDTH), lambda i: (i, 0)),
                  pl.BlockSpec((None, N_MEM, MEM_WIDTH), lambda i: (layer, i // tiles_per_seq, 0)),
                  pl.BlockSpec((None, N_MEM, MEM_WIDTH), lambda i: (layer, i // tiles_per_seq, 0)),
                  pl.BlockSpec((None, 1, MEM_HEAD_DIM), lambda i: (layer, 0, 0)),
                  pl.BlockSpec((None, MAIN_WIDTH, d), lambda i: (layer, 0, 0)),
                  pl.BlockSpec((None, MEM_WIDTH, d), lambda i: (layer, MAIN_WIDTH // MEM_WIDTH, 0))],
        out_specs=pl.BlockSpec((TM, d), lambda i: (i, 0)),
        out_shape=jax.ShapeDtypeStruct((t, d), F32),
        compiler_params=_params(("parallel",)),
        name="mixer_out",
    )(x, main, mq, mk, mv, q_gain, w_out, w_out)


def _rope_body(pos_ref, freq_ref, c_ref, s_ref):
    ang = pos_ref[...].astype(F32) * freq_ref[...]
    lane = lax.broadcasted_iota(jnp.int32, ang.shape, 1)
    half = MLA_ROPE_DIM // 2
    cos = jnp.cos(ang)
    sin = jnp.sin(ang)
    c_ref[...] = jnp.where(lane < MLA_ROPE_DIM, cos, 0.0)
    s_ref[...] = jnp.where(lane < half, -sin, jnp.where(lane < MLA_ROPE_DIM, sin, 0.0))


def _rope_tables(positions):
    t = positions.size
    tr = min(TROPE, t)
    half = MLA_ROPE_DIM // 2
    inv_freq = ROPE_THETA ** (-jnp.arange(0, MLA_ROPE_DIM, 2, dtype=F32) / MLA_ROPE_DIM)
    freq = jnp.tile(inv_freq, LANES // half).reshape(1, LANES)
    shp = jax.ShapeDtypeStruct((t, LANES), F32)
    return pl.pallas_call(
        _rope_body,
        grid=(t // tr,),
        in_specs=[pl.BlockSpec((tr, 1), lambda i: (i, 0)),
                  pl.BlockSpec((1, LANES), lambda i: (0, 0))],
        out_specs=[pl.BlockSpec((tr, LANES), lambda i: (i, 0)),
                   pl.BlockSpec((tr, LANES), lambda i: (i, 0))],
        out_shape=[shp, shp],
        compiler_params=_params(("parallel",)),
        name="rope_tables",
    )(positions.reshape(t, 1), freq)


def _rope(u, cmul, smul):
    return u * cmul + (pltpu.roll(u, 96, 1) + pltpu.roll(u, 32, 1)) * smul


def _kv_body(h_ref, wd_ref, lg_ref, wu_ref, kgn_ref, kgr_ref, c_ref, s_ref, k_ref, v_ref):
    low = _dot(h_ref[...], wd_ref[...])
    ckv = _rms(low[:, :KV_LORA_RANK], lg_ref[...]).astype(BF16)
    kv = _dot(ckv, wu_ref[...])
    pe = low[:, KV_LORA_RANK:]
    pe_ss = jnp.sum(pe * pe, axis=-1, keepdims=True)
    pe_rot = _rope(pe * kgr_ref[...], c_ref[...], s_ref[...])
    kgn = kgn_ref[...]
    for h in range(MLA_HEADS):
        kn = kv[:, h * HEAD_PAD:h * HEAD_PAD + MLA_NOPE_DIM]
        ms = (jnp.sum(kn * kn, axis=-1, keepdims=True) + pe_ss) * (1.0 / MLA_QK_DIM)
        rinv = lax.rsqrt(ms + EPS)
        k_ref[0, h, :, 0:MLA_NOPE_DIM] = ((kn * rinv) * kgn).astype(k_ref.dtype)
        k_ref[0, h, :, MLA_NOPE_DIM:HEAD_PAD] = (pe_rot * rinv).astype(k_ref.dtype)
        v_ref[0, h, :, 0:MLA_V_DIM] = kv[:, h * HEAD_PAD + MLA_NOPE_DIM:(h + 1) * HEAD_PAD].astype(v_ref.dtype)
        v_ref[0, h, :, MLA_V_DIM:V_EXT] = jnp.ones((kv.shape[0], V_EXT - MLA_V_DIM), v_ref.dtype)


def _shared_kv(h, w_dkv_pad, latent_gain, w_ukv, kg_nope, kg_rope_pad, cmul, smul, bsz):
    t, d = h.shape
    seq = t // bsz
    tps = seq // TM
    return pl.pallas_call(
        _kv_body,
        grid=(t // TM,),
        in_specs=[pl.BlockSpec((TM, d), lambda i: (i, 0)),
                  pl.BlockSpec((d, KV_DOWN_PAD), lambda i: (0, 0)),
                  pl.BlockSpec((1, KV_LORA_RANK), lambda i: (0, 0)),
                  pl.BlockSpec((KV_LORA_RANK, MLA_HEADS * HEAD_PAD), lambda i: (0, 0)),
                  pl.BlockSpec((1, MLA_NOPE_DIM), lambda i: (0, 0)),
                  pl.BlockSpec((1, LANES), lambda i: (0, 0)),
                  pl.BlockSpec((TM, LANES), lambda i: (i, 0)),
                  pl.BlockSpec((TM, LANES), lambda i: (i, 0))],
        out_specs=[pl.BlockSpec((1, MLA_HEADS, TM, HEAD_PAD), lambda i: (i // tps, 0, i % tps, 0)),
                   pl.BlockSpec((1, MLA_HEADS, TM, V_EXT), lambda i: (i // tps, 0, i % tps, 0))],
        out_shape=[jax.ShapeDtypeStruct((bsz, MLA_HEADS, seq, HEAD_PAD), BF16),
                   jax.ShapeDtypeStruct((bsz, MLA_HEADS, seq, V_EXT), BF16)],
        compiler_params=_params(("parallel",)),
        name="shared_kv",
    )(h, w_dkv_pad, latent_gain, w_ukv, kg_nope, kg_rope_pad, cmul, smul)


def _q_body(h_ref, wb_ref, lg_ref, wu_ref, qg_ref, c_ref, s_ref, mq_ref, q_ref, qa_scr, qb_scr):
    s = pl.program_id(0)
    c = (MLA_QK_DIM ** -0.5) * LOG2_E

    @pl.when(s == 0)
    def _():
        qb_scr[...] = jnp.zeros(qb_scr.shape, F32)

    def step(q_new, q_old):
        low = _dot(h_ref[...], wb_ref[...])
        mq_ref[...] = low[:, Q_LORA_PAD:]
        cq = _rms(low[:, :Q_LORA_PAD], lg_ref[...], n=Q_LORA_RANK).astype(BF16)
        q_new[...] = _dot(cq, wu_ref[...])
        qg = qg_ref[...]
        cmul = c_ref[...]
        smul = s_ref[...]
        for h in range(MLA_HEADS):
            qh = _rms(q_old[:, h * HEAD_PAD:(h + 1) * HEAD_PAD], qg, n=MLA_QK_DIM)
            q_ref[0, h, :, 0:MLA_NOPE_DIM] = (qh[:, :MLA_NOPE_DIM] * c).astype(q_ref.dtype)
            q_ref[0, h, :, MLA_NOPE_DIM:HEAD_PAD] = (_rope(qh[:, MLA_NOPE_DIM:], cmul, smul) * c).astype(q_ref.dtype)

    @pl.when(s % 2 == 0)
    def _():
        step(qa_scr, qb_scr)

    @pl.when(s % 2 == 1)
    def _():
        step(qb_scr, qa_scr)


def _mla_q(h, w_b_pad, latent_gain_pad, w_uq_pad, q_gain_pad, cmul, smul, bsz):
    t, d = h.shape
    seq = t // bsz
    tps = seq // TM
    nt = t // TM
    cur = lambda s: jnp.minimum(s, nt - 1)
    prev = lambda s: jnp.maximum(s - 1, 0)
    return pl.pallas_call(
        _q_body,
        grid=(nt + 1,),
        in_specs=[pl.BlockSpec((TM, d), lambda s: (cur(s), 0)),
                  _resident((d, Q_LORA_PAD + MEM_WIDTH)),
                  _resident((1, Q_LORA_PAD)),
                  _resident((Q_LORA_PAD, MLA_HEADS * HEAD_PAD)),
                  _resident((1, HEAD_PAD)),
                  pl.BlockSpec((TM, LANES), lambda s: (prev(s), 0)),
                  pl.BlockSpec((TM, LANES), lambda s: (prev(s), 0))],
        out_specs=[pl.BlockSpec((TM, MEM_WIDTH), lambda s: (cur(s), 0)),
                   pl.BlockSpec((1, MLA_HEADS, TM, HEAD_PAD), lambda s: (prev(s) // tps, 0, prev(s) % tps, 0))],
        out_shape=[jax.ShapeDtypeStruct((t, MEM_WIDTH), F32),
                   jax.ShapeDtypeStruct((bsz, MLA_HEADS, seq, HEAD_PAD), BF16)],
        scratch_shapes=[pltpu.VMEM((TM, MLA_HEADS * HEAD_PAD), F32),
                        pltpu.VMEM((TM, MLA_HEADS * HEAD_PAD), F32)],
        compiler_params=_params(("arbitrary",)),
        name="proj_mla_q",
    )(h, w_b_pad, latent_gain_pad, w_uq_pad, q_gain_pad, cmul, smul)


def _flash_body(qi_tab, ki_tab, last_tab, q_ref, k_ref, v_ref, o_ref, m_scr, acc_scr):
    p_id = pl.program_id(1)
    diag = qi_tab[p_id] == ki_tab[p_id]
    reps = TQ // LANES

    def scores(h):
        return _dot_nt(q_ref[0, h], k_ref[0, h])

    @pl.when(diag)
    def _():
        r_i = lax.broadcasted_iota(jnp.int32, (TQ, TQ), 0)
        c_i = lax.broadcasted_iota(jnp.int32, (TQ, TQ), 1)
        keep = c_i <= r_i
        for h in range(MLA_HEADS):
            s = jnp.where(keep, scores(h), -jnp.inf)
            m = jnp.broadcast_to(jnp.max(s, axis=-1, keepdims=True), (TQ, LANES))
            e = jnp.exp2(s - _lane_tile(m, reps))
            m_scr[h] = m
            acc_scr[h] = _dot(e.astype(BF16), v_ref[0, h])

    @pl.when(jnp.logical_not(diag))
    def _():
        for h in range(MLA_HEADS):
            s = scores(h)
            m_prev = m_scr[h]
            m = jnp.maximum(m_prev, jnp.max(s, axis=-1, keepdims=True))
            alpha = jnp.exp2(m_prev - m)
            e = jnp.exp2(s - _lane_tile(m, reps))
            m_scr[h] = m
            acc_scr[h] = _lane_tile(alpha, V_EXT // LANES) * acc_scr[h] + _dot(e.astype(BF16), v_ref[0, h])

    @pl.when(last_tab[p_id] == 1)
    def _():
        for h in range(MLA_HEADS):
            acc = acc_scr[h]
            o_ref[:, h * MLA_V_DIM:(h + 1) * MLA_V_DIM] = (
                acc[:, :MLA_V_DIM] * (1.0 / acc[:, MLA_V_DIM:])).astype(o_ref.dtype)


def _flash(q, k, v):
    bsz, nh, seq, _ = q.shape
    nq = seq // TQ
    qi, ki, last = [], [], []
    for a in range(nq):
        order = [a] + list(range(a))
        for n, b in enumerate(order):
            qi.append(a)
            ki.append(b)
            last.append(1 if n == len(order) - 1 else 0)
    npairs = len(qi)
    tabs = [jnp.asarray(x, jnp.int32) for x in (qi, ki, last)]
    grid_spec = pltpu.PrefetchScalarGridSpec(
        num_scalar_prefetch=3,
        grid=(bsz, npairs),
        in_specs=[pl.BlockSpec((1, nh, TQ, HEAD_PAD), lambda b, p, qt, kt, lt: (b, 0, qt[p], 0)),
                  pl.BlockSpec((1, nh, TQ, HEAD_PAD), lambda b, p, qt, kt, lt: (b, 0, kt[p], 0)),
                  pl.BlockSpec((1, nh, TQ, V_EXT), lambda b, p, qt, kt, lt: (b, 0, kt[p], 0))],
        out_specs=pl.BlockSpec((TQ, nh * MLA_V_DIM), lambda b, p, qt, kt, lt: (b * nq + qt[p], 0)),
        scratch_shapes=[pltpu.VMEM((nh, TQ, LANES), F32),
                        pltpu.VMEM((nh, TQ, V_EXT), F32)],
    )
    return pl.pallas_call(
        _flash_body,
        grid_spec=grid_spec,
        out_shape=jax.ShapeDtypeStruct((bsz * seq, nh * MLA_V_DIM), BF16),
        compiler_params=_params(("parallel", "arbitrary")),
        name="mla_attention",
    )(*tabs, q, k, v)


def _pad_heads(w, nh, dh, dh_pad):
    lead = w.shape[:-1]
    w = w.reshape(lead + (nh, dh))
    w = jnp.pad(w, [(0, 0)] * len(lead) + [(0, 0), (0, dh_pad - dh)])
    return w.reshape(lead + (nh * dh_pad,))


def _pad_last(w, n):
    return jnp.pad(w, [(0, 0)] * (w.ndim - 1) + [(0, n - w.shape[-1])])


def kernel(x, mem, positions, ffn1_gain, ffn1_w_in, ffn1_w_out, mix_gain, w_out, mem_gain, w_mem_kv, mem_q_gain, mem_k_gain, a_w_in, a_b_gates, a_conv, a_head_gain, kv_gain, w_dkv, kv_latent_gain, w_ukv, k_gain, b_w_in, b_q_latent_gain, b_w_uq, b_q_gain, ffn2_gain, ffn2_w_in, ffn2_w_out):
    bsz, seq, d = x.shape
    t = bsz * seq
    assert d == D_MODEL and seq % TM == 0 and seq % CHUNK == 0 and seq % TQ == 0 and t % min(TROPE, t) == 0 and t % TM_FFN == 0

    ffn_w = (ffn1_w_in[0].astype(BF16), ffn1_w_out[0].astype(BF16))
    w_out_b = w_out.astype(BF16)

    nqk = MLSTM_HEADS * MLSTM_QK_DIM
    c0, c1, c2, c3 = 2 * nqk, 2 * nqk + MAIN_WIDTH, 2 * nqk + 2 * MAIN_WIDTH, 2 * nqk + 2 * MAIN_WIDTH + 2 * MLSTM_HEADS

    cmul, smul = _rope_tables(positions)
    mk, mv = _mem_kv(mem.reshape(bsz * N_MEM, d), mem_gain, w_mem_kv.astype(BF16), mem_k_gain)

    xs = x.reshape(t, d)
    k_sh = v_sh = None
    for layer in range(DEPTH):
        xs, hmix, ffn_w = _ffn(xs, ffn1_gain[layer], *ffn_w, next_gain=mix_gain[layer],
                               next_weights=(ffn2_w_in, ffn2_w_out, layer))
        if layer < N_A_LAYERS:
            w_in = a_w_in[layer]
            w_qk = jnp.concatenate([_pad_heads(w_in[:, :nqk], MLSTM_HEADS, MLSTM_QK_DIM, QK_PAD),
                                    _pad_heads(w_in[:, nqk:c0], MLSTM_HEADS, MLSTM_QK_DIM, QK_PAD)], axis=-1)
            conv_w = jnp.concatenate([_pad_heads(a_conv[layer][:, :nqk], MLSTM_HEADS, MLSTM_QK_DIM, QK_PAD),
                                      _pad_heads(a_conv[layer][:, nqk:], MLSTM_HEADS, MLSTM_QK_DIM, QK_PAD)], axis=-1)
            w_g = w_in[:, c2:c3]
            qk, v, og, mq, g, gt = _proj_a(
                hmix, w_qk.astype(BF16), w_in[:, c0:c1].astype(BF16), w_in[:, c1:c2].astype(BF16),
                w_in[:, c3:].astype(BF16), _pad_last(w_g, GATE_PAD).astype(BF16),
                jnp.pad(w_g.T, ((0, 2 * SUBLANES - 2 * MLSTM_HEADS), (0, 0))).astype(BF16), a_b_gates[layer])
            main = _mlstm(qk, conv_w, v, og, g, gt, a_head_gain[layer], bsz)
        else:
            j = layer - N_A_LAYERS
            w_in = b_w_in[j]
            w_b = jnp.concatenate([_pad_last(w_in[:, :Q_LORA_RANK], Q_LORA_PAD), w_in[:, Q_LORA_RANK:]], axis=-1)
            w_uq = jnp.pad(_pad_heads(b_w_uq[j], MLA_HEADS, MLA_QK_DIM, HEAD_PAD),
                           ((0, Q_LORA_PAD - Q_LORA_RANK), (0, 0))).astype(BF16)
            mq, q = _mla_q(hmix, w_b.astype(BF16), _pad_last(b_q_latent_gain[j], Q_LORA_PAD).reshape(1, Q_LORA_PAD),
                           w_uq, _pad_last(b_q_gain[j], HEAD_PAD).reshape(1, HEAD_PAD), cmul, smul, bsz)
            main = _flash(q, k_sh, v_sh)
        xs = _mixer_out(xs, main, mq, mk, mv, mem_q_gain.reshape(DEPTH, 1, MEM_HEAD_DIM), w_out_b, layer, seq)
        following = (ffn1_w_in, ffn1_w_out, layer + 1) if layer + 1 < DEPTH else None
        xs, hkv, ffn_w = _ffn(xs, ffn2_gain[layer], *ffn_w, next_gain=kv_gain if layer == N_A_LAYERS - 1 else None,
                              next_weights=following)
        if layer == N_A_LAYERS - 1:
            k_sh, v_sh = _shared_kv(hkv, _pad_last(w_dkv, KV_DOWN_PAD).astype(BF16),
                                    kv_latent_gain.reshape(1, KV_LORA_RANK), w_ukv.astype(BF16),
                                    k_gain[:MLA_NOPE_DIM].reshape(1, MLA_NOPE_DIM),
                                    _pad_last(k_gain[MLA_NOPE_DIM:], LANES).reshape(1, LANES), cmul, smul, bsz)
    return xs.reshape(bsz, seq, d)
```

```python
import functools

import jax
import jax.numpy as jnp
from jax import lax
from jax.experimental import pallas as pl
from jax.experimental.pallas import tpu as pltpu

F32 = jnp.float32
BF16 = jnp.bfloat16

D_MODEL = 2048
DEPTH = 4
N_MEM = 256
N_A_LAYERS = DEPTH // 2
EPS = 1e-6
MEM_HEADS = 4
MEM_WIDTH = D_MODEL // 4
MEM_HEAD_DIM = MEM_WIDTH // MEM_HEADS
MAIN_WIDTH = D_MODEL - MEM_WIDTH
MLSTM_HEADS = 4
MLSTM_V_DIM = MAIN_WIDTH // MLSTM_HEADS
MLSTM_QK_DIM = MLSTM_V_DIM // 2
CONV_WIDTH = 4
MLA_NOPE_DIM = 128
MLA_ROPE_DIM = 64
MLA_QK_DIM = MLA_NOPE_DIM + MLA_ROPE_DIM
MLA_V_DIM = 128
MLA_HEADS = MAIN_WIDTH // MLA_V_DIM
Q_LORA_RANK = 448
KV_LORA_RANK = 512
ROPE_THETA = 10000.0
D_FF = 5632
LOG2_E = 1.4426950408889634

LANES = 128
SUBLANES = 8
BF16_ROWS = 2 * SUBLANES
V7X_VMEM_BYTES = 64 * 1024 * 1024

QK_PAD = 256
Q_LORA_PAD = 512
HEAD_PAD = 256
GATE_PAD = LANES
KV_DOWN_PAD = KV_LORA_RANK + LANES
V_EXT = MLA_V_DIM + LANES

TM = 512
TM_FFN = 1024
TF = 512
CHUNK = 256
TQ = 512
TROPE = 2048
MIB = 1024 * 1024
VMEM_LIMIT = 56 * MIB
VMEM_LIMIT_FFN = 58 * MIB
assert VMEM_LIMIT_FFN < V7X_VMEM_BYTES


def _params(sem, vmem_limit=VMEM_LIMIT):
    return pltpu.CompilerParams(dimension_semantics=sem, vmem_limit_bytes=vmem_limit)


def _rms(x, gain, n=None):
    if n is None:
        ms = jnp.mean(x * x, axis=-1, keepdims=True)
    else:
        ms = jnp.sum(x * x, axis=-1, keepdims=True) * (1.0 / n)
    return (x * lax.rsqrt(ms + EPS)) * gain


def _log_sigmoid(x):
    return -(jnp.maximum(-x, 0.0) + jnp.log1p(jnp.exp(-jnp.abs(x))))


def _lane_tile(x, reps):
    return jnp.concatenate([x] * reps, axis=1)


def _dot(a, b):
    return jnp.dot(a, b, preferred_element_type=F32)


def _dot_nt(a, b):
    return lax.dot_general(a, b, (((1,), (1,)), ((), ())), preferred_element_type=F32)


def _dot_tn(a, b):
    return lax.dot_general(a, b, (((0,), (0,)), ((), ())), preferred_element_type=F32)


def _ffn_body(emit_next, cast_next, x_hbm, g_ref, wg_ref, wu_ref, wo_ref, *rest):
    rest = list(rest)
    ng_ref = rest.pop(0) if emit_next else None
    nwi_ref, nwo_ref = (rest.pop(0), rest.pop(0)) if cast_next else (None, None)
    o_ref = rest.pop(0)
    hn_ref = rest.pop(0) if emit_next else None
    nwi_out, nwo_out = (rest.pop(0), rest.pop(0)) if cast_next else (None, None)
    h_scr, x_buf, x_sem = rest
    i = pl.program_id(0)
    j = pl.program_id(1)

    if cast_next:
        nwi_out[...] = nwi_ref[...].astype(BF16)
        nwo_out[...] = nwo_ref[...].astype(BF16)

    def x_copy(tile):
        rows = pl.ds(pl.multiple_of(tile * TM_FFN, TM_FFN), TM_FFN)
        return pltpu.make_async_copy(x_hbm.at[rows, :], x_buf, x_sem)

    @pl.when(j == 0)
    def _():
        @pl.when(i == 0)
        def _():
            x_copy(0).start()

        x_copy(i).wait()
        x = x_buf[...]
        h_scr[...] = _rms(x, g_ref[...]).astype(BF16)
        o_ref[...] = x

    @pl.when(jnp.logical_and(j == 1, i + 1 < pl.num_programs(0)))
    def _():
        x_copy(i + 1).start()

    h = h_scr[...]
    g = _dot(h, wg_ref[...])
    u = _dot(h, wu_ref[...])
    a = ((g * jax.nn.sigmoid(g)) * u) * 0.5
    o_ref[...] += _dot(a.astype(BF16), wo_ref[...])

    if emit_next:
        @pl.when(j == pl.num_programs(1) - 1)
        def _():
            hn_ref[...] = _rms(o_ref[...], ng_ref[...]).astype(BF16)


def _ffn(x, gain, w_in, w_out, next_gain=None, next_weights=None):
    t, d = x.shape
    nf = D_FF // TF
    ni = t // TM_FFN
    assert nf >= 2 and t % TM_FFN == 0
    emit = next_gain is not None
    cast = next_weights is not None
    in_specs = [
        pl.BlockSpec(memory_space=pl.ANY),
        pl.BlockSpec((1, d), lambda i, j: (0, 0)),
        pl.BlockSpec((d, TF), lambda i, j: (0, j)),
        pl.BlockSpec((d, TF), lambda i, j: (0, j + nf)),
        pl.BlockSpec((TF, d), lambda i, j: (j, 0)),
    ]
    args = [x, gain.reshape(1, d), w_in, w_in, w_out]
    out_shape = [jax.ShapeDtypeStruct((t, d), F32)]
    out_specs = [pl.BlockSpec((TM_FFN, d), lambda i, j: (i, 0))]
    if emit:
        in_specs.append(pl.BlockSpec((1, d), lambda i, j: (0, 0)))
        args.append(next_gain.reshape(1, d))
        out_shape.append(jax.ShapeDtypeStruct((t, d), BF16))
        out_specs.append(pl.BlockSpec((TM_FFN, d), lambda i, j: (i, 0)))
    if cast:
        nwi, nwo, nl = next_weights
        ri, ci = d // ni, 2 * D_FF // nf
        ro = D_FF // (ni * nf)
        assert d % ni == 0 and ri % BF16_ROWS == 0 and ci % LANES == 0
        assert D_FF % (ni * nf) == 0 and ro % BF16_ROWS == 0
        in_specs += [pl.BlockSpec((None, ri, ci), lambda i, j: (nl, i, j)),
                     pl.BlockSpec((None, ro, d), lambda i, j: (nl, i * nf + j, 0))]
        args += [nwi, nwo]
        out_shape += [jax.ShapeDtypeStruct((d, 2 * D_FF), BF16), jax.ShapeDtypeStruct((D_FF, d), BF16)]
        out_specs += [pl.BlockSpec((ri, ci), lambda i, j: (i, j)),
                      pl.BlockSpec((ro, d), lambda i, j: (i * nf + j, 0))]
    res = pl.pallas_call(
        functools.partial(_ffn_body, emit, cast),
        grid=(ni, nf),
        in_specs=in_specs,
        out_specs=out_specs,
        out_shape=out_shape,
        scratch_shapes=[pltpu.VMEM((TM_FFN, d), BF16),
                        pltpu.VMEM((TM_FFN, d), F32),
                        pltpu.SemaphoreType.DMA(())],
        compiler_params=_params(("arbitrary", "arbitrary"), VMEM_LIMIT_FFN),
        name="ffn" + ("_next" if emit else "") + ("_cast" if cast else ""),
    )(*args)
    x_new = res[0]
    hn = res[1] if emit else None
    next_bf16 = (res[-2], res[-1]) if cast else None
    return x_new, hn, next_bf16


def _resident(shape):
    return pl.BlockSpec(shape, lambda i: (0,) * len(shape), pipeline_mode=pl.Buffered(1))


def _proja_body(h_ref, wqk_ref, wv_ref, wo_ref, wmq_ref, wg_ref, wgt_ref, brow_ref, bcol_ref,
                qk_ref, v_ref, og_ref, mq_ref, g_ref, gt_ref):
    h = h_ref[...]
    qk_ref[...] = _dot(h, wqk_ref[...])
    v_ref[...] = _dot(h, wv_ref[...]).astype(v_ref.dtype)
    og_ref[...] = _dot(h, wo_ref[...])
    mq_ref[...] = _dot(h, wmq_ref[...])
    g = _dot(h, wg_ref[...]) + brow_ref[...]
    lane = lax.broadcasted_iota(jnp.int32, g.shape, 1)
    g_ref[...] = jnp.where(lane >= MLSTM_HEADS, _log_sigmoid(g), g)
    gt = _dot_nt(wgt_ref[...], h)[:2 * MLSTM_HEADS] + bcol_ref[...]
    row = lax.broadcasted_iota(jnp.int32, gt.shape, 0)
    gt_ref[...] = jnp.where(row >= MLSTM_HEADS, _log_sigmoid(gt), gt)


def _proj_a(h, w_qk, w_v, w_o, w_mq, w_g, w_gt, bias):
    t, d = h.shape
    ng = 2 * MLSTM_HEADS
    nqk = w_qk.shape[1]
    brow = jnp.zeros((1, GATE_PAD), F32).at[0, :ng].set(bias)
    bcol = bias.reshape(ng, 1)
    row_spec = lambda n: pl.BlockSpec((TM, n), lambda i: (i, 0))
    return pl.pallas_call(
        _proja_body,
        grid=(t // TM,),
        in_specs=[row_spec(d), _resident((d, nqk)), _resident((d, MAIN_WIDTH)), _resident((d, MAIN_WIDTH)),
                  _resident((d, MEM_WIDTH)), _resident((d, GATE_PAD)), _resident((BF16_ROWS, d)),
                  _resident((1, GATE_PAD)), _resident((ng, 1))],
        out_specs=[row_spec(nqk), row_spec(MAIN_WIDTH), row_spec(MAIN_WIDTH), row_spec(MEM_WIDTH),
                   row_spec(GATE_PAD), pl.BlockSpec((ng, TM), lambda i: (0, i))],
        out_shape=[jax.ShapeDtypeStruct((t, nqk), F32),
                   jax.ShapeDtypeStruct((t, MAIN_WIDTH), BF16),
                   jax.ShapeDtypeStruct((t, MAIN_WIDTH), F32),
                   jax.ShapeDtypeStruct((t, MEM_WIDTH), F32),
                   jax.ShapeDtypeStruct((t, GATE_PAD), F32),
                   jax.ShapeDtypeStruct((ng, t), F32)],
        compiler_params=_params(("parallel",)),
        name="proj_mlstm",
    )(h, w_qk, w_v, w_o, w_mq, w_g, w_gt, brow, bcol)


def _mlstm_body(qk_ref, cw_ref, v_ref, og_ref, g_ref, gt_ref, hg_ref, out_ref, xbuf, c_scr, m_scr):
    nh, dkp, dv = MLSTM_HEADS, QK_PAD, MLSTM_V_DIM
    ell = CHUNK
    halo = SUBLANES

    @pl.when(pl.program_id(1) == 0)
    def _():
        xbuf[0:halo, :] = jnp.zeros((halo, xbuf.shape[1]), F32)
        c_scr[...] = jnp.zeros(c_scr.shape, F32)
        m_scr[...] = jnp.zeros(m_scr.shape, F32)

    xbuf[halo:halo + ell, :] = qk_ref[...]
    cw = cw_ref[...]
    y = cw[CONV_WIDTH - 1:CONV_WIDTH, :] * xbuf[halo:halo + ell, :]
    for j in range(CONV_WIDTH - 1):
        sh = CONV_WIDTH - 1 - j
        y = y + cw[j:j + 1, :] * xbuf[halo - sh:halo - sh + ell, :]
    xbuf[0:halo, :] = xbuf[ell:ell + halo, :]
    y = y * jax.nn.sigmoid(y)

    r_i = lax.broadcasted_iota(jnp.int32, (ell, ell), 0)
    c_i = lax.broadcasted_iota(jnp.int32, (ell, ell), 1)
    causal = c_i <= r_i
    tri = jnp.where(causal, 1.0, 0.0).astype(F32)
    tri_t = jnp.where(r_i <= c_i, 1.0, 0.0).astype(F32)
    gcol = g_ref[...]
    grow = gt_ref[...]
    bcol_all = jnp.dot(tri, gcol, preferred_element_type=F32, precision=lax.Precision.HIGHEST)
    brow_all = jnp.dot(grow, tri_t, preferred_element_type=F32, precision=lax.Precision.HIGHEST)

    lane = lax.broadcasted_iota(jnp.int32, (ell, LANES), 1)
    ones_col = jnp.where(lane == 0, 1.0, 0.0).astype(BF16)
    v_all = v_ref[...]
    og = og_ref[...]
    hg = hg_ref[...]

    for h in range(nh):
        q = (y[:, h * dkp:(h + 1) * dkp] * (MLSTM_QK_DIM ** -0.5)).astype(BF16)
        k_f = y[:, (nh + h) * dkp:(nh + h + 1) * dkp]
        k = k_f.astype(BF16)
        v_ext = jnp.concatenate([v_all[:, h * dv:(h + 1) * dv], ones_col], axis=-1)
        ig_col = gcol[:, h:h + 1]
        ig_row = grow[h:h + 1, :]
        b_col = bcol_all[:, nh + h:nh + h + 1]
        b_row = brow_all[nh + h:nh + h + 1, :]
        m_prev = m_scr[h][:, 0:1]
        c_prev = c_scr[h]

        log_w = jnp.where(causal, b_col - b_row + ig_row, -jnp.inf)
        log_inter = b_col + m_prev
        m_t = jnp.maximum(jnp.max(log_w, axis=-1, keepdims=True), log_inter)
        w = jnp.exp(log_w - m_t)
        w_inter = jnp.exp(log_inter - m_t)
        s = _dot_nt(q, k) * w
        inter = _dot(q, c_prev.astype(BF16))
        num = _dot(s.astype(BF16), v_ext[:, :dv]) + w_inter * inter[:, :dv]
        den = jnp.sum(s, axis=-1, keepdims=True) + w_inter * inter[:, dv:dv + 1]
        hh = num * (1.0 / jnp.maximum(jnp.abs(den), jnp.exp(-m_t)))

        g_last = b_col[ell - 1:ell, :]
        log_a = g_last - b_col + ig_col
        m_new = jnp.maximum(g_last + m_prev, jnp.max(log_a, axis=0, keepdims=True))
        decay = jnp.exp(g_last + m_prev - m_new)
        a = jnp.exp(log_a - m_new)
        c_scr[h] = decay * c_prev + _dot_tn((k_f * a).astype(BF16), v_ext)
        m_scr[h] = jnp.broadcast_to(m_new, (1, LANES))

        hn = _rms(hh, hg[:, h * dv:(h + 1) * dv])
        o_h = og[:, h * dv:(h + 1) * dv]
        out_ref[:, h * dv:(h + 1) * dv] = (jax.nn.sigmoid(o_h) * hn).astype(out_ref.dtype)


def _mlstm(qk, conv_w, v, og, g, gt, head_gain, bsz):
    t = qk.shape[0]
    nc = t // bsz // CHUNK
    nqk = qk.shape[1]
    return pl.pallas_call(
        _mlstm_body,
        grid=(bsz, nc),
        in_specs=[pl.BlockSpec((CHUNK, nqk), lambda b, c: (b * nc + c, 0)),
                  pl.BlockSpec((CONV_WIDTH, nqk), lambda b, c: (0, 0)),
                  pl.BlockSpec((CHUNK, MAIN_WIDTH), lambda b, c: (b * nc + c, 0)),
                  pl.BlockSpec((CHUNK, MAIN_WIDTH), lambda b, c: (b * nc + c, 0)),
                  pl.BlockSpec((CHUNK, GATE_PAD), lambda b, c: (b * nc + c, 0)),
                  pl.BlockSpec((2 * MLSTM_HEADS, CHUNK), lambda b, c: (0, b * nc + c)),
                  pl.BlockSpec((1, MAIN_WIDTH), lambda b, c: (0, 0))],
        out_specs=pl.BlockSpec((CHUNK, MAIN_WIDTH), lambda b, c: (b * nc + c, 0)),
        out_shape=jax.ShapeDtypeStruct((t, MAIN_WIDTH), BF16),
        scratch_shapes=[pltpu.VMEM((CHUNK + SUBLANES, nqk), F32),
                        pltpu.VMEM((MLSTM_HEADS, QK_PAD, MLSTM_V_DIM + LANES), F32),
                        pltpu.VMEM((MLSTM_HEADS, 1, LANES), F32)],
        compiler_params=_params(("parallel", "arbitrary")),
        name="mlstm",
    )(qk, conv_w, v, og, g, gt, head_gain.reshape(1, MAIN_WIDTH))


def _memkv_body(mem_ref, g_ref, w_ref, kg_ref, mk_ref, mv_ref):
    hn = _rms(mem_ref[...], g_ref[...]).astype(BF16)
    kv = _dot(hn, w_ref[...])
    kg = kg_ref[...]
    for h in range(MEM_HEADS):
        sl = slice(h * MEM_HEAD_DIM, (h + 1) * MEM_HEAD_DIM)
        mk_ref[:, sl] = _rms(kv[:, sl], kg).astype(mk_ref.dtype)
    mv_ref[...] = kv[:, MEM_WIDTH:].astype(mv_ref.dtype)


def _mem_kv(mem2d, mem_gain, w_mem_kv, mem_k_gain):
    rows = mem2d.shape[0]
    shp = jax.ShapeDtypeStruct((DEPTH, rows, MEM_WIDTH), BF16)
    return pl.pallas_call(
        _memkv_body,
        grid=(DEPTH,),
        in_specs=[pl.BlockSpec((rows, D_MODEL), lambda l: (0, 0)),
                  pl.BlockSpec((None, 1, D_MODEL), lambda l: (l, 0, 0)),
                  pl.BlockSpec((None, D_MODEL, 2 * MEM_WIDTH), lambda l: (l, 0, 0)),
                  pl.BlockSpec((None, 1, MEM_HEAD_DIM), lambda l: (l, 0, 0))],
        out_specs=[pl.BlockSpec((None, rows, MEM_WIDTH), lambda l: (l, 0, 0)),
                   pl.BlockSpec((None, rows, MEM_WIDTH), lambda l: (l, 0, 0))],
        out_shape=[shp, shp],
        compiler_params=_params(("parallel",)),
        name="mem_kv",
    )(mem2d, mem_gain.reshape(DEPTH, 1, D_MODEL), w_mem_kv, mem_k_gain.reshape(DEPTH, 1, MEM_HEAD_DIM))


def _mix_body(x_ref, main_ref, mq_ref, mk_ref, mv_ref, qg_ref, wmain_ref, wmem_ref, o_ref):
    mq = mq_ref[...]
    mk = mk_ref[...]
    mv = mv_ref[...]
    qg = qg_ref[...]
    heads = []
    for h in range(MEM_HEADS):
        sl = slice(h * MEM_HEAD_DIM, (h + 1) * MEM_HEAD_DIM)
        qn = _rms(mq[:, sl], qg).astype(BF16)
        s = _dot_nt(qn, mk[:, sl]) * (MEM_HEAD_DIM ** -0.5)
        e = jnp.exp(s - jnp.max(s, axis=-1, keepdims=True))
        oh = _dot(e.astype(BF16), mv[:, sl]) * (1.0 / jnp.sum(e, axis=-1, keepdims=True))
        heads.append(oh.astype(BF16))
    mem_out = jnp.concatenate(heads, axis=-1)
    o_ref[...] = x_ref[...] + _dot(main_ref[...], wmain_ref[...]) + _dot(mem_out, wmem_ref[...])


def _mixer_out(x, main, mq, mk, mv, q_gain, w_out, layer, seq):
    t, d = x.shape
    tiles_per_seq = seq // TM
    return pl.pallas_call(
        _mix_body,
        grid=(t // TM,),
        in_specs=[pl.BlockSpec((TM, d), lambda i: (i, 0)),
                  pl.BlockSpec((TM, MAIN_WIDTH), lambda i: (i, 0)),
                  pl.BlockSpec((TM, MEM_WIDTH), lambda i: (i, 0)),
                  pl.BlockSpec((None, N_MEM, MEM_WIDTH), lambda i: (layer, i // tiles_per_seq, 0)),
                  pl.BlockSpec((None, N_MEM, MEM_WIDTH), lambda i: (layer, i // tiles_per_seq, 0)),
                  pl.BlockSpec((None, 1, MEM_HEAD_DIM), lambda i: (layer, 0, 0)),
                  pl.BlockSpec((None, MAIN_WIDTH, d), lambda i: (layer, 0, 0)),
                  pl.BlockSpec((None, MEM_WIDTH, d), lambda i: (layer, MAIN_WIDTH // MEM_WIDTH, 0))],
        out_specs=pl.BlockSpec((TM, d), lambda i: (i, 0)),
        out_shape=jax.ShapeDtypeStruct((t, d), F32),
        compiler_params=_params(("parallel",)),
        name="mixer_out",
    )(x, main, mq, mk, mv, q_gain, w_out, w_out)


def _rope_body(pos_ref, freq_ref, c_ref, s_ref):
    ang = pos_ref[...].astype(F32) * freq_ref[...]
    lane = lax.broadcasted_iota(jnp.int32, ang.shape, 1)
    half = MLA_ROPE_DIM // 2
    cos = jnp.cos(ang)
    sin = jnp.sin(ang)
    c_ref[...] = jnp.where(lane < MLA_ROPE_DIM, cos, 0.0)
    s_ref[...] = jnp.where(lane < half, -sin, jnp.where(lane < MLA_ROPE_DIM, sin, 0.0))


def _rope_tables(positions):
    t = positions.size
    tr = min(TROPE, t)
    half = MLA_ROPE_DIM // 2
    inv_freq = ROPE_THETA ** (-jnp.arange(0, MLA_ROPE_DIM, 2, dtype=F32) / MLA_ROPE_DIM)
    freq = jnp.tile(inv_freq, LANES // half).reshape(1, LANES)
    shp = jax.ShapeDtypeStruct((t, LANES), F32)
    return pl.pallas_call(
        _rope_body,
        grid=(t // tr,),
        in_specs=[pl.BlockSpec((tr, 1), lambda i: (i, 0)),
                  pl.BlockSpec((1, LANES), lambda i: (0, 0))],
        out_specs=[pl.BlockSpec((tr, LANES), lambda i: (i, 0)),
                   pl.BlockSpec((tr, LANES), lambda i: (i, 0))],
        out_shape=[shp, shp],
        compiler_params=_params(("parallel",)),
        name="rope_tables",
    )(positions.reshape(t, 1), freq)


def _rope(u, cmul, smul):
    half = MLA_ROPE_DIM // 2
    return u * cmul + (pltpu.roll(u, LANES - half, 1) + pltpu.roll(u, half, 1)) * smul


def _kv_body(h_ref, wd_ref, lg_ref, wu_ref, kgn_ref, kgr_ref, c_ref, s_ref, k_ref, v_ref):
    low = _dot(h_ref[...], wd_ref[...])
    ckv = _rms(low[:, :KV_LORA_RANK], lg_ref[...]).astype(BF16)
    kv = _dot(ckv, wu_ref[...])
    pe = low[:, KV_LORA_RANK:]
    pe_ss = jnp.sum(pe * pe, axis=-1, keepdims=True)
    pe_rot = _rope(pe * kgr_ref[...], c_ref[...], s_ref[...])
    kgn = kgn_ref[...]
    for h in range(MLA_HEADS):
        kn = kv[:, h * HEAD_PAD:h * HEAD_PAD + MLA_NOPE_DIM]
        ms = (jnp.sum(kn * kn, axis=-1, keepdims=True) + pe_ss) * (1.0 / MLA_QK_DIM)
        rinv = lax.rsqrt(ms + EPS)
        k_ref[0, h, :, 0:MLA_NOPE_DIM] = ((kn * rinv) * kgn).astype(k_ref.dtype)
        k_ref[0, h, :, MLA_NOPE_DIM:HEAD_PAD] = (pe_rot * rinv).astype(k_ref.dtype)
        v_ref[0, h, :, 0:MLA_V_DIM] = kv[:, h * HEAD_PAD + MLA_NOPE_DIM:(h + 1) * HEAD_PAD].astype(v_ref.dtype)
        v_ref[0, h, :, MLA_V_DIM:V_EXT] = jnp.ones((kv.shape[0], V_EXT - MLA_V_DIM), v_ref.dtype)


def _shared_kv(h, w_dkv_pad, latent_gain, w_ukv, kg_nope, kg_rope_pad, cmul, smul, bsz):
    t, d = h.shape
    seq = t // bsz
    tps = seq // TM
    return pl.pallas_call(
        _kv_body,
        grid=(t // TM,),
        in_specs=[pl.BlockSpec((TM, d), lambda i: (i, 0)),
                  pl.BlockSpec((d, KV_DOWN_PAD), lambda i: (0, 0)),
                  pl.BlockSpec((1, KV_LORA_RANK), lambda i: (0, 0)),
                  pl.BlockSpec((KV_LORA_RANK, MLA_HEADS * HEAD_PAD), lambda i: (0, 0)),
                  pl.BlockSpec((1, MLA_NOPE_DIM), lambda i: (0, 0)),
                  pl.BlockSpec((1, LANES), lambda i: (0, 0)),
                  pl.BlockSpec((TM, LANES), lambda i: (i, 0)),
                  pl.BlockSpec((TM, LANES), lambda i: (i, 0))],
        out_specs=[pl.BlockSpec((1, MLA_HEADS, TM, HEAD_PAD), lambda i: (i // tps, 0, i % tps, 0)),
                   pl.BlockSpec((1, MLA_HEADS, TM, V_EXT), lambda i: (i // tps, 0, i % tps, 0))],
        out_shape=[jax.ShapeDtypeStruct((bsz, MLA_HEADS, seq, HEAD_PAD), BF16),
                   jax.ShapeDtypeStruct((bsz, MLA_HEADS, seq, V_EXT), BF16)],
        compiler_params=_params(("parallel",)),
        name="shared_kv",
    )(h, w_dkv_pad, latent_gain, w_ukv, kg_nope, kg_rope_pad, cmul, smul)


def _q_body(h_ref, wb_ref, lg_ref, wu_ref, qg_ref, c_ref, s_ref, mq_ref, q_ref, qa_scr, qb_scr):
    s = pl.program_id(0)
    c = (MLA_QK_DIM ** -0.5) * LOG2_E

    @pl.when(s == 0)
    def _():
        qb_scr[...] = jnp.zeros(qb_scr.shape, F32)

    def step(q_new, q_old):
        low = _dot(h_ref[...], wb_ref[...])
        mq_ref[...] = low[:, Q_LORA_PAD:]
        cq = _rms(low[:, :Q_LORA_PAD], lg_ref[...], n=Q_LORA_RANK).astype(BF16)
        q_new[...] = _dot(cq, wu_ref[...])
        qg = qg_ref[...]
        cmul = c_ref[...]
        smul = s_ref[...]
        for h in range(MLA_HEADS):
            qh = _rms(q_old[:, h * HEAD_PAD:(h + 1) * HEAD_PAD], qg, n=MLA_QK_DIM)
            q_ref[0, h, :, 0:MLA_NOPE_DIM] = (qh[:, :MLA_NOPE_DIM] * c).astype(q_ref.dtype)
            q_ref[0, h, :, MLA_NOPE_DIM:HEAD_PAD] = (_rope(qh[:, MLA_NOPE_DIM:], cmul, smul) * c).astype(q_ref.dtype)

    @pl.when(s % 2 == 0)
    def _():
        step(qa_scr, qb_scr)

    @pl.when(s % 2 == 1)
    def _():
        step(qb_scr, qa_scr)


def _mla_q(h, w_b_pad, latent_gain_pad, w_uq_pad, q_gain_pad, cmul, smul, bsz):
    t, d = h.shape
    seq = t // bsz
    tps = seq // TM
    nt = t // TM
    cur = lambda s: jnp.minimum(s, nt - 1)
    prev = lambda s: jnp.maximum(s - 1, 0)
    return pl.pallas_call(
        _q_body,
        grid=(nt + 1,),
        in_specs=[pl.BlockSpec((TM, d), lambda s: (cur(s), 0)),
                  _resident((d, Q_LORA_PAD + MEM_WIDTH)),
                  _resident((1, Q_LORA_PAD)),
                  _resident((Q_LORA_PAD, MLA_HEADS * HEAD_PAD)),
                  _resident((1, HEAD_PAD)),
                  pl.BlockSpec((TM, LANES), lambda s: (prev(s), 0)),
                  pl.BlockSpec((TM, LANES), lambda s: (prev(s), 0))],
        out_specs=[pl.BlockSpec((TM, MEM_WIDTH), lambda s: (cur(s), 0)),
                   pl.BlockSpec((1, MLA_HEADS, TM, HEAD_PAD), lambda s: (prev(s) // tps, 0, prev(s) % tps, 0))],
        out_shape=[jax.ShapeDtypeStruct((t, MEM_WIDTH), F32),
                   jax.ShapeDtypeStruct((bsz, MLA_HEADS, seq, HEAD_PAD), BF16)],
        scratch_shapes=[pltpu.VMEM((TM, MLA_HEADS * HEAD_PAD), F32),
                        pltpu.VMEM((TM, MLA_HEADS * HEAD_PAD), F32)],
        compiler_params=_params(("arbitrary",)),
        name="proj_mla_q",
    )(h, w_b_pad, latent_gain_pad, w_uq_pad, q_gain_pad, cmul, smul)


def _flash_body(qi_tab, ki_tab, last_tab, q_ref, k_ref, v_ref, o_ref, m_scr, acc_scr):
    p_id = pl.program_id(1)
    diag = qi_tab[p_id] == ki_tab[p_id]
    reps = TQ // LANES

    def scores(h):
        return _dot_nt(q_ref[0, h], k_ref[0, h])

    @pl.when(diag)
    def _():
        r_i = lax.broadcasted_iota(jnp.int32, (TQ, TQ), 0)
        c_i = lax.broadcasted_iota(jnp.int32, (TQ, TQ), 1)
        keep = c_i <= r_i
        for h in range(MLA_HEADS):
            s = jnp.where(keep, scores(h), -jnp.inf)
            m = jnp.broadcast_to(jnp.max(s, axis=-1, keepdims=True), (TQ, LANES))
            e = jnp.exp2(s - _lane_tile(m, reps))
            m_scr[h] = m
            acc_scr[h] = _dot(e.astype(BF16), v_ref[0, h])

    @pl.when(jnp.logical_not(diag))
    def _():
        for h in range(MLA_HEADS):
            s = scores(h)
            m_prev = m_scr[h]
            m = jnp.maximum(m_prev, jnp.max(s, axis=-1, keepdims=True))
            alpha = jnp.exp2(m_prev - m)
            e = jnp.exp2(s - _lane_tile(m, reps))
            m_scr[h] = m
            acc_scr[h] = _lane_tile(alpha, V_EXT // LANES) * acc_scr[h] + _dot(e.astype(BF16), v_ref[0, h])

    @pl.when(last_tab[p_id] == 1)
    def _():
        for h in range(MLA_HEADS):
            acc = acc_scr[h]
            o_ref[:, h * MLA_V_DIM:(h + 1) * MLA_V_DIM] = (
                acc[:, :MLA_V_DIM] * (1.0 / acc[:, MLA_V_DIM:])).astype(o_ref.dtype)


def _flash(q, k, v):
    bsz, nh, seq, _ = q.shape
    nq = seq // TQ
    qi, ki, last = [], [], []
    for a in range(nq):
        order = [a] + list(range(a))
        for n, b in enumerate(order):
            qi.append(a)
            ki.append(b)
            last.append(1 if n == len(order) - 1 else 0)
    npairs = len(qi)
    tabs = [jnp.asarray(x, jnp.int32) for x in (qi, ki, last)]
    grid_spec = pltpu.PrefetchScalarGridSpec(
        num_scalar_prefetch=3,
        grid=(bsz, npairs),
        in_specs=[pl.BlockSpec((1, nh, TQ, HEAD_PAD), lambda b, p, qt, kt, lt: (b, 0, qt[p], 0)),
                  pl.BlockSpec((1, nh, TQ, HEAD_PAD), lambda b, p, qt, kt, lt: (b, 0, kt[p], 0)),
                  pl.BlockSpec((1, nh, TQ, V_EXT), lambda b, p, qt, kt, lt: (b, 0, kt[p], 0))],
        out_specs=pl.BlockSpec((TQ, nh * MLA_V_DIM), lambda b, p, qt, kt, lt: (b * nq + qt[p], 0)),
        scratch_shapes=[pltpu.VMEM((nh, TQ, LANES), F32),
                        pltpu.VMEM((nh, TQ, V_EXT), F32)],
    )
    return pl.pallas_call(
        _flash_body,
        grid_spec=grid_spec,
        out_shape=jax.ShapeDtypeStruct((bsz * seq, nh * MLA_V_DIM), BF16),
        compiler_params=_params(("parallel", "arbitrary")),
        name="mla_attention",
    )(*tabs, q, k, v)


def _pad_heads(w, nh, dh, dh_pad):
    lead = w.shape[:-1]
    w = w.reshape(lead + (nh, dh))
    w = jnp.pad(w, [(0, 0)] * len(lead) + [(0, 0), (0, dh_pad - dh)])
    return w.reshape(lead + (nh * dh_pad,))


def _pad_last(w, n):
    return jnp.pad(w, [(0, 0)] * (w.ndim - 1) + [(0, n - w.shape[-1])])


def kernel(x, mem, positions, ffn1_gain, ffn1_w_in, ffn1_w_out, mix_gain, w_out, mem_gain, w_mem_kv, mem_q_gain, mem_k_gain, a_w_in, a_b_gates, a_conv, a_head_gain, kv_gain, w_dkv, kv_latent_gain, w_ukv, k_gain, b_w_in, b_q_latent_gain, b_w_uq, b_q_gain, ffn2_gain, ffn2_w_in, ffn2_w_out):
    bsz, seq, d = x.shape
    t = bsz * seq
    assert d == D_MODEL and seq % TM == 0 and seq % CHUNK == 0 and seq % TQ == 0 and t % min(TROPE, t) == 0 and t % TM_FFN == 0

    ffn_w = (ffn1_w_in[0].astype(BF16), ffn1_w_out[0].astype(BF16))
    w_out_b = w_out.astype(BF16)

    nqk = MLSTM_HEADS * MLSTM_QK_DIM
    c0, c1, c2, c3 = 2 * nqk, 2 * nqk + MAIN_WIDTH, 2 * nqk + 2 * MAIN_WIDTH, 2 * nqk + 2 * MAIN_WIDTH + 2 * MLSTM_HEADS

    cmul, smul = _rope_tables(positions)
    mk, mv = _mem_kv(mem.reshape(bsz * N_MEM, d), mem_gain, w_mem_kv.astype(BF16), mem_k_gain)

    xs = x.reshape(t, d)
    k_sh = v_sh = None
    for layer in range(DEPTH):
        xs, hmix, ffn_w = _ffn(xs, ffn1_gain[layer], *ffn_w, next_gain=mix_gain[layer],
                               next_weights=(ffn2_w_in, ffn2_w_out, layer))
        if layer < N_A_LAYERS:
            w_in = a_w_in[layer]
            w_qk = jnp.concatenate([_pad_heads(w_in[:, :nqk], MLSTM_HEADS, MLSTM_QK_DIM, QK_PAD),
                                    _pad_heads(w_in[:, nqk:c0], MLSTM_HEADS, MLSTM_QK_DIM, QK_PAD)], axis=-1)
            conv_w = jnp.concatenate([_pad_heads(a_conv[layer][:, :nqk], MLSTM_HEADS, MLSTM_QK_DIM, QK_PAD),
                                      _pad_heads(a_conv[layer][:, nqk:], MLSTM_HEADS, MLSTM_QK_DIM, QK_PAD)], axis=-1)
            w_g = w_in[:, c2:c3]
            qk, v, og, mq, g, gt = _proj_a(
                hmix, w_qk.astype(BF16), w_in[:, c0:c1].astype(BF16), w_in[:, c1:c2].astype(BF16),
                w_in[:, c3:].astype(BF16), _pad_last(w_g, GATE_PAD).astype(BF16),
                jnp.pad(w_g.T, ((0, BF16_ROWS - 2 * MLSTM_HEADS), (0, 0))).astype(BF16), a_b_gates[layer])
            main = _mlstm(qk, conv_w, v, og, g, gt, a_head_gain[layer], bsz)
        else:
            j = layer - N_A_LAYERS
            w_in = b_w_in[j]
            w_b = jnp.concatenate([_pad_last(w_in[:, :Q_LORA_RANK], Q_LORA_PAD), w_in[:, Q_LORA_RANK:]], axis=-1)
            w_uq = jnp.pad(_pad_heads(b_w_uq[j], MLA_HEADS, MLA_QK_DIM, HEAD_PAD),
                           ((0, Q_LORA_PAD - Q_LORA_RANK), (0, 0))).astype(BF16)
            mq, q = _mla_q(hmix, w_b.astype(BF16), _pad_last(b_q_latent_gain[j], Q_LORA_PAD).reshape(1, Q_LORA_PAD),
                           w_uq, _pad_last(b_q_gain[j], HEAD_PAD).reshape(1, HEAD_PAD), cmul, smul, bsz)
            main = _flash(q, k_sh, v_sh)
        xs = _mixer_out(xs, main, mq, mk, mv, mem_q_gain.reshape(DEPTH, 1, MEM_HEAD_DIM), w_out_b, layer, seq)
        following = (ffn1_w_in, ffn1_w_out, layer + 1) if layer + 1 < DEPTH else None
        xs, hkv, ffn_w = _ffn(xs, ffn2_gain[layer], *ffn_w, next_gain=kv_gain if layer == N_A_LAYERS - 1 else None,
                              next_weights=following)
        if layer == N_A_LAYERS - 1:
            k_sh, v_sh = _shared_kv(hkv, _pad_last(w_dkv, KV_DOWN_PAD).astype(BF16),
                                    kv_latent_gain.reshape(1, KV_LORA_RANK), w_ukv.astype(BF16),
                                    k_gain[:MLA_NOPE_DIM].reshape(1, MLA_NOPE_DIM),
                                    _pad_last(k_gain[MLA_NOPE_DIM:], LANES).reshape(1, LANES), cmul, smul, bsz)
    return xs.reshape(bsz, seq, d)
```

```python
import functools

import jax
import jax.numpy as jnp
from jax import lax
from jax.experimental import pallas as pl
from jax.experimental.pallas import tpu as pltpu

F32 = jnp.float32
BF16 = jnp.bfloat16

D_MODEL = 2048
DEPTH = 4
N_MEM = 256
N_A_LAYERS = DEPTH // 2
EPS = 1e-6
MEM_HEADS = 4
MEM_WIDTH = D_MODEL // 4
MEM_HEAD_DIM = MEM_WIDTH // MEM_HEADS
MAIN_WIDTH = D_MODEL - MEM_WIDTH
MLSTM_HEADS = 4
MLSTM_V_DIM = MAIN_WIDTH // MLSTM_HEADS
MLSTM_QK_DIM = MLSTM_V_DIM // 2
CONV_WIDTH = 4
MLA_NOPE_DIM = 128
MLA_ROPE_DIM = 64
MLA_QK_DIM = MLA_NOPE_DIM + MLA_ROPE_DIM
MLA_V_DIM = 128
MLA_HEADS = MAIN_WIDTH // MLA_V_DIM
Q_LORA_RANK = 448
KV_LORA_RANK = 512
ROPE_THETA = 10000.0
D_FF = 5632
LOG2_E = 1.4426950408889634

LANES = 128
SUBLANES = 8
BF16_ROWS = 2 * SUBLANES
V7X_VMEM_BYTES = 64 * 1024 * 1024

QK_PAD = 256
Q_LORA_PAD = 512
HEAD_PAD = 256
GATE_PAD = LANES
KV_DOWN_PAD = KV_LORA_RANK + LANES
V_EXT = MLA_V_DIM + LANES

TM = 512
TM_FFN = 1024
TF = 512
CHUNK = 256
TQ = 512
TROPE = 2048
MIB = 1024 * 1024
VMEM_LIMIT = 56 * MIB
VMEM_LIMIT_FFN = 58 * MIB
assert VMEM_LIMIT_FFN < V7X_VMEM_BYTES


def _params(sem, vmem_limit=VMEM_LIMIT):
    return pltpu.CompilerParams(dimension_semantics=sem, vmem_limit_bytes=vmem_limit)


def _rms(x, gain, n=None):
    if n is None:
        ms = jnp.mean(x * x, axis=-1, keepdims=True)
    else:
        ms = jnp.sum(x * x, axis=-1, keepdims=True) * (1.0 / n)
    return (x * lax.rsqrt(ms + EPS)) * gain


def _log_sigmoid(x):
    return -(jnp.maximum(-x, 0.0) + jnp.log1p(jnp.exp(-jnp.abs(x))))


def _lane_tile(x, reps):
    return jnp.concatenate([x] * reps, axis=1)


def _dot(a, b):
    return jnp.dot(a, b, preferred_element_type=F32)


def _dot_nt(a, b):
    return lax.dot_general(a, b, (((1,), (1,)), ((), ())), preferred_element_type=F32)


def _dot_tn(a, b):
    return lax.dot_general(a, b, (((0,), (0,)), ((), ())), preferred_element_type=F32)


def _ffn_body(emit_next, cast_next, x_hbm, g_ref, wg_ref, wu_ref, wo_ref, *rest):
    rest = list(rest)
    ng_ref = rest.pop(0) if emit_next else None
    nwi_ref, nwo_ref = (rest.pop(0), rest.pop(0)) if cast_next else (None, None)
    o_ref = rest.pop(0)
    hn_ref = rest.pop(0) if emit_next else None
    nwi_out, nwo_out = (rest.pop(0), rest.pop(0)) if cast_next else (None, None)
    h_scr, x_buf, x_sem = rest
    i = pl.program_id(0)
    j = pl.program_id(1)

    def x_copy(tile):
        rows = pl.ds(pl.multiple_of(tile * TM_FFN, TM_FFN), TM_FFN)
        return pltpu.make_async_copy(x_hbm.at[rows, :], x_buf, x_sem)

    def swiglu_chunk():
        if cast_next:
            nwi_out[...] = nwi_ref[...].astype(BF16)
            nwo_out[...] = nwo_ref[...].astype(BF16)
        h = h_scr[...]
        g = _dot(h, wg_ref[...])
        u = _dot(h, wu_ref[...])
        a = ((g * jax.nn.sigmoid(g)) * u) * 0.5
        return _dot(a.astype(BF16), wo_ref[...])

    @pl.when(j == 0)
    def _():
        @pl.when(i == 0)
        def _():
            x_copy(0).start()

        x_copy(i).wait()
        h_scr[...] = _rms(x_buf[...], g_ref[...]).astype(BF16)
        o_ref[...] = x_buf[...] + swiglu_chunk()

    @pl.when(jnp.logical_and(j == 1, i + 1 < pl.num_programs(0)))
    def _():
        x_copy(i + 1).start()

    @pl.when(j > 0)
    def _():
        o_ref[...] += swiglu_chunk()

    if emit_next:
        @pl.when(j == pl.num_programs(1) - 1)
        def _():
            hn_ref[...] = _rms(o_ref[...], ng_ref[...]).astype(BF16)


def _ffn(x, gain, w_in, w_out, next_gain=None, next_weights=None):
    t, d = x.shape
    nf = D_FF // TF
    ni = t // TM_FFN
    assert nf >= 2 and t % TM_FFN == 0
    emit = next_gain is not None
    cast = next_weights is not None
    in_specs = [
        pl.BlockSpec(memory_space=pl.ANY),
        pl.BlockSpec((1, d), lambda i, j: (0, 0)),
        pl.BlockSpec((d, TF), lambda i, j: (0, j)),
        pl.BlockSpec((d, TF), lambda i, j: (0, j + nf)),
        pl.BlockSpec((TF, d), lambda i, j: (j, 0)),
    ]
    args = [x, gain.reshape(1, d), w_in, w_in, w_out]
    out_shape = [jax.ShapeDtypeStruct((t, d), F32)]
    out_specs = [pl.BlockSpec((TM_FFN, d), lambda i, j: (i, 0))]
    if emit:
        in_specs.append(pl.BlockSpec((1, d), lambda i, j: (0, 0)))
        args.append(next_gain.reshape(1, d))
        out_shape.append(jax.ShapeDtypeStruct((t, d), BF16))
        out_specs.append(pl.BlockSpec((TM_FFN, d), lambda i, j: (i, 0)))
    if cast:
        nwi, nwo, nl = next_weights
        ri, ci = d // ni, 2 * D_FF // nf
        ro = D_FF // (ni * nf)
        assert d % ni == 0 and ri % BF16_ROWS == 0 and ci % LANES == 0
        assert D_FF % (ni * nf) == 0 and ro % BF16_ROWS == 0
        in_specs += [pl.BlockSpec((None, ri, ci), lambda i, j: (nl, i, j)),
                     pl.BlockSpec((None, ro, d), lambda i, j: (nl, i * nf + j, 0))]
        args += [nwi, nwo]
        out_shape += [jax.ShapeDtypeStruct((d, 2 * D_FF), BF16), jax.ShapeDtypeStruct((D_FF, d), BF16)]
        out_specs += [pl.BlockSpec((ri, ci), lambda i, j: (i, j)),
                      pl.BlockSpec((ro, d), lambda i, j: (i * nf + j, 0))]
    res = pl.pallas_call(
        functools.partial(_ffn_body, emit, cast),
        grid=(ni, nf),
        in_specs=in_specs,
        out_specs=out_specs,
        out_shape=out_shape,
        scratch_shapes=[pltpu.VMEM((TM_FFN, d), BF16),
                        pltpu.VMEM((TM_FFN, d), F32),
                        pltpu.SemaphoreType.DMA(())],
        compiler_params=_params(("arbitrary", "arbitrary"), VMEM_LIMIT_FFN),
        name="ffn" + ("_next" if emit else "") + ("_cast" if cast else ""),
    )(*args)
    x_new = res[0]
    hn = res[1] if emit else None
    next_bf16 = (res[-2], res[-1]) if cast else None
    return x_new, hn, next_bf16


def _resident(shape):
    return pl.BlockSpec(shape, lambda i: (0,) * len(shape), pipeline_mode=pl.Buffered(1))


def _proja_body(h_ref, wqk_ref, wv_ref, wo_ref, wmq_ref, wg_ref, wgt_ref, brow_ref, bcol_ref,
                qk_ref, v_ref, og_ref, mq_ref, g_ref, gt_ref):
    h = h_ref[...]
    qk_ref[...] = _dot(h, wqk_ref[...])
    v_ref[...] = _dot(h, wv_ref[...]).astype(v_ref.dtype)
    og_ref[...] = _dot(h, wo_ref[...])
    mq_ref[...] = _dot(h, wmq_ref[...])
    g = _dot(h, wg_ref[...]) + brow_ref[...]
    lane = lax.broadcasted_iota(jnp.int32, g.shape, 1)
    g_ref[...] = jnp.where(lane >= MLSTM_HEADS, _log_sigmoid(g), g)
    gt = _dot_nt(wgt_ref[...], h)[:2 * MLSTM_HEADS] + bcol_ref[...]
    row = lax.broadcasted_iota(jnp.int32, gt.shape, 0)
    gt_ref[...] = jnp.where(row >= MLSTM_HEADS, _log_sigmoid(gt), gt)


def _proj_a(h, w_qk, w_v, w_o, w_mq, w_g, w_gt, bias):
    t, d = h.shape
    ng = 2 * MLSTM_HEADS
    nqk = w_qk.shape[1]
    brow = jnp.zeros((1, GATE_PAD), F32).at[0, :ng].set(bias)
    bcol = bias.reshape(ng, 1)
    row_spec = lambda n: pl.BlockSpec((TM, n), lambda i: (i, 0))
    return pl.pallas_call(
        _proja_body,
        grid=(t // TM,),
        in_specs=[row_spec(d), _resident((d, nqk)), _resident((d, MAIN_WIDTH)), _resident((d, MAIN_WIDTH)),
                  _resident((d, MEM_WIDTH)), _resident((d, GATE_PAD)), _resident((BF16_ROWS, d)),
                  _resident((1, GATE_PAD)), _resident((ng, 1))],
        out_specs=[row_spec(nqk), row_spec(MAIN_WIDTH), row_spec(MAIN_WIDTH), row_spec(MEM_WIDTH),
                   row_spec(GATE_PAD), pl.BlockSpec((ng, TM), lambda i: (0, i))],
        out_shape=[jax.ShapeDtypeStruct((t, nqk), F32),
                   jax.ShapeDtypeStruct((t, MAIN_WIDTH), BF16),
                   jax.ShapeDtypeStruct((t, MAIN_WIDTH), F32),
                   jax.ShapeDtypeStruct((t, MEM_WIDTH), F32),
                   jax.ShapeDtypeStruct((t, GATE_PAD), F32),
                   jax.ShapeDtypeStruct((ng, t), F32)],
        compiler_params=_params(("parallel",)),
        name="proj_mlstm",
    )(h, w_qk, w_v, w_o, w_mq, w_g, w_gt, brow, bcol)


def _mlstm_body(qk_ref, cw_ref, v_ref, og_ref, g_ref, gt_ref, hg_ref, out_ref, xbuf, c_scr, m_scr):
    nh, dkp, dv = MLSTM_HEADS, QK_PAD, MLSTM_V_DIM
    ell = CHUNK
    halo = SUBLANES

    @pl.when(pl.program_id(1) == 0)
    def _():
        xbuf[0:halo, :] = jnp.zeros((halo, xbuf.shape[1]), F32)
        c_scr[...] = jnp.zeros(c_scr.shape, F32)
        m_scr[...] = jnp.zeros(m_scr.shape, F32)

    xbuf[halo:halo + ell, :] = qk_ref[...]
    cw = cw_ref[...]
    y = cw[CONV_WIDTH - 1:CONV_WIDTH, :] * xbuf[halo:halo + ell, :]
    for j in range(CONV_WIDTH - 1):
        sh = CONV_WIDTH - 1 - j
        y = y + cw[j:j + 1, :] * xbuf[halo - sh:halo - sh + ell, :]
    xbuf[0:halo, :] = xbuf[ell:ell + halo, :]
    y = y * jax.nn.sigmoid(y)

    r_i = lax.broadcasted_iota(jnp.int32, (ell, ell), 0)
    c_i = lax.broadcasted_iota(jnp.int32, (ell, ell), 1)
    causal = c_i <= r_i
    tri = jnp.where(causal, 1.0, 0.0).astype(F32)
    tri_t = jnp.where(r_i <= c_i, 1.0, 0.0).astype(F32)
    gcol = g_ref[...]
    grow = gt_ref[...]
    bcol_all = jnp.dot(tri, gcol, preferred_element_type=F32, precision=lax.Precision.HIGHEST)
    brow_all = jnp.dot(grow, tri_t, preferred_element_type=F32, precision=lax.Precision.HIGHEST)

    lane = lax.broadcasted_iota(jnp.int32, (ell, LANES), 1)
    ones_col = jnp.where(lane == 0, 1.0, 0.0).astype(BF16)
    v_all = v_ref[...]
    og = og_ref[...]
    hg = hg_ref[...]

    for h in range(nh):
        q = (y[:, h * dkp:(h + 1) * dkp] * (MLSTM_QK_DIM ** -0.5)).astype(BF16)
        k_f = y[:, (nh + h) * dkp:(nh + h + 1) * dkp]
        k = k_f.astype(BF16)
        v_ext = jnp.concatenate([v_all[:, h * dv:(h + 1) * dv], ones_col], axis=-1)
        ig_col = gcol[:, h:h + 1]
        ig_row = grow[h:h + 1, :]
        b_col = bcol_all[:, nh + h:nh + h + 1]
        b_row = brow_all[nh + h:nh + h + 1, :]
        m_prev = m_scr[h][:, 0:1]
        c_prev = c_scr[h]

        log_w = jnp.where(causal, b_col - b_row + ig_row, -jnp.inf)
        log_inter = b_col + m_prev
        m_t = jnp.maximum(jnp.max(log_w, axis=-1, keepdims=True), log_inter)
        w = jnp.exp(log_w - m_t)
        w_inter = jnp.exp(log_inter - m_t)
        s = _dot_nt(q, k) * w
        inter = _dot(q, c_prev.astype(BF16))
        num = _dot(s.astype(BF16), v_ext[:, :dv]) + w_inter * inter[:, :dv]
        den = jnp.sum(s, axis=-1, keepdims=True) + w_inter * inter[:, dv:dv + 1]
        hh = num * (1.0 / jnp.maximum(jnp.abs(den), jnp.exp(-m_t)))

        g_last = b_col[ell - 1:ell, :]
        log_a = g_last - b_col + ig_col
        m_new = jnp.maximum(g_last + m_prev, jnp.max(log_a, axis=0, keepdims=True))
        decay = jnp.exp(g_last + m_prev - m_new)
        a = jnp.exp(log_a - m_new)
        c_scr[h] = decay * c_prev + _dot_tn((k_f * a).astype(BF16), v_ext)
        m_scr[h] = jnp.broadcast_to(m_new, (1, LANES))

        hn = _rms(hh, hg[:, h * dv:(h + 1) * dv])
        o_h = og[:, h * dv:(h + 1) * dv]
        out_ref[:, h * dv:(h + 1) * dv] = (jax.nn.sigmoid(o_h) * hn).astype(out_ref.dtype)


def _mlstm(qk, conv_w, v, og, g, gt, head_gain, bsz):
    t = qk.shape[0]
    nc = t // bsz // CHUNK
    nqk = qk.shape[1]
    return pl.pallas_call(
        _mlstm_body,
        grid=(bsz, nc),
        in_specs=[pl.BlockSpec((CHUNK, nqk), lambda b, c: (b * nc + c, 0)),
                  pl.BlockSpec((CONV_WIDTH, nqk), lambda b, c: (0, 0)),
                  pl.BlockSpec((CHUNK, MAIN_WIDTH), lambda b, c: (b * nc + c, 0)),
                  pl.BlockSpec((CHUNK, MAIN_WIDTH), lambda b, c: (b * nc + c, 0)),
                  pl.BlockSpec((CHUNK, GATE_PAD), lambda b, c: (b * nc + c, 0)),
                  pl.BlockSpec((2 * MLSTM_HEADS, CHUNK), lambda b, c: (0, b * nc + c)),
                  pl.BlockSpec((1, MAIN_WIDTH), lambda b, c: (0, 0))],
        out_specs=pl.BlockSpec((CHUNK, MAIN_WIDTH), lambda b, c: (b * nc + c, 0)),
        out_shape=jax.ShapeDtypeStruct((t, MAIN_WIDTH), BF16),
        scratch_shapes=[pltpu.VMEM((CHUNK + SUBLANES, nqk), F32),
                        pltpu.VMEM((MLSTM_HEADS, QK_PAD, MLSTM_V_DIM + LANES), F32),
                        pltpu.VMEM((MLSTM_HEADS, 1, LANES), F32)],
        compiler_params=_params(("parallel", "arbitrary")),
        name="mlstm",
    )(qk, conv_w, v, og, g, gt, head_gain.reshape(1, MAIN_WIDTH))


def _memkv_body(mem_ref, g_ref, w_ref, kg_ref, mk_ref, mv_ref):
    hn = _rms(mem_ref[...], g_ref[...]).astype(BF16)
    kv = _dot(hn, w_ref[...])
    kg = kg_ref[...]
    for h in range(MEM_HEADS):
        sl = slice(h * MEM_HEAD_DIM, (h + 1) * MEM_HEAD_DIM)
        mk_ref[:, sl] = _rms(kv[:, sl], kg).astype(mk_ref.dtype)
    mv_ref[...] = kv[:, MEM_WIDTH:].astype(mv_ref.dtype)


def _mem_kv(mem2d, mem_gain, w_mem_kv, mem_k_gain):
    rows = mem2d.shape[0]
    shp = jax.ShapeDtypeStruct((DEPTH, rows, MEM_WIDTH), BF16)
    return pl.pallas_call(
        _memkv_body,
        grid=(DEPTH,),
        in_specs=[pl.BlockSpec((rows, D_MODEL), lambda l: (0, 0)),
                  pl.BlockSpec((None, 1, D_MODEL), lambda l: (l, 0, 0)),
                  pl.BlockSpec((None, D_MODEL, 2 * MEM_WIDTH), lambda l: (l, 0, 0)),
                  pl.BlockSpec((None, 1, MEM_HEAD_DIM), lambda l: (l, 0, 0))],
        out_specs=[pl.BlockSpec((None, rows, MEM_WIDTH), lambda l: (l, 0, 0)),
                   pl.BlockSpec((None, rows, MEM_WIDTH), lambda l: (l, 0, 0))],
        out_shape=[shp, shp],
        compiler_params=_params(("parallel",)),
        name="mem_kv",
    )(mem2d, mem_gain.reshape(DEPTH, 1, D_MODEL), w_mem_kv, mem_k_gain.reshape(DEPTH, 1, MEM_HEAD_DIM))


def _mix_body(x_ref, main_ref, mq_ref, mk_ref, mv_ref, qg_ref, wmain_ref, wmem_ref, o_ref):
    mq = mq_ref[...]
    mk = mk_ref[...]
    mv = mv_ref[...]
    qg = qg_ref[...]
    heads = []
    for h in range(MEM_HEADS):
        sl = slice(h * MEM_HEAD_DIM, (h + 1) * MEM_HEAD_DIM)
        qn = _rms(mq[:, sl], qg).astype(BF16)
        s = _dot_nt(qn, mk[:, sl]) * (MEM_HEAD_DIM ** -0.5)
        e = jnp.exp(s - jnp.max(s, axis=-1, keepdims=True))
        oh = _dot(e.astype(BF16), mv[:, sl]) * (1.0 / jnp.sum(e, axis=-1, keepdims=True))
        heads.append(oh.astype(BF16))
    mem_out = jnp.concatenate(heads, axis=-1)
    o_ref[...] = x_ref[...] + _dot(main_ref[...], wmain_ref[...]) + _dot(mem_out, wmem_ref[...])


def _mixer_out(x, main, mq, mk, mv, q_gain, w_out, layer, seq):
    t, d = x.shape
    tiles_per_seq = seq // TM
    return pl.pallas_call(
        _mix_body,
        grid=(t // TM,),
        in_specs=[pl.BlockSpec((TM, d), lambda i: (i, 0)),
                  pl.BlockSpec((TM, MAIN_WIDTH), lambda i: (i, 0)),
                  pl.BlockSpec((TM, MEM_WIDTH), lambda i: (i, 0)),
                  pl.BlockSpec((None, N_MEM, MEM_WIDTH), lambda i: (layer, i // tiles_per_seq, 0)),
                  pl.BlockSpec((None, N_MEM, MEM_WIDTH), lambda i: (layer, i // tiles_per_seq, 0)),
                  pl.BlockSpec((None, 1, MEM_HEAD_DIM), lambda i: (layer, 0, 0)),
                  pl.BlockSpec((None, MAIN_WIDTH, d), lambda i: (layer, 0, 0)),
                  pl.BlockSpec((None, MEM_WIDTH, d), lambda i: (layer, MAIN_WIDTH // MEM_WIDTH, 0))],
        out_specs=pl.BlockSpec((TM, d), lambda i: (i, 0)),
        out_shape=jax.ShapeDtypeStruct((t, d), F32),
        compiler_params=_params(("parallel",)),
        name="mixer_out",
    )(x, main, mq, mk, mv, q_gain, w_out, w_out)


def _rope_body(pos_ref, freq_ref, c_ref, s_ref):
    ang = pos_ref[...].astype(F32) * freq_ref[...]
    lane = lax.broadcasted_iota(jnp.int32, ang.shape, 1)
    half = MLA_ROPE_DIM // 2
    cos = jnp.cos(ang)
    sin = jnp.sin(ang)
    c_ref[...] = jnp.where(lane < MLA_ROPE_DIM, cos, 0.0)
    s_ref[...] = jnp.where(lane < half, -sin, jnp.where(lane < MLA_ROPE_DIM, sin, 0.0))


def _rope_tables(positions):
    t = positions.size
    tr = min(TROPE, t)
    half = MLA_ROPE_DIM // 2
    inv_freq = ROPE_THETA ** (-jnp.arange(0, MLA_ROPE_DIM, 2, dtype=F32) / MLA_ROPE_DIM)
    freq = jnp.tile(inv_freq, LANES // half).reshape(1, LANES)
    shp = jax.ShapeDtypeStruct((t, LANES), F32)
    return pl.pallas_call(
        _rope_body,
        grid=(t // tr,),
        in_specs=[pl.BlockSpec((tr, 1), lambda i: (i, 0)),
                  pl.BlockSpec((1, LANES), lambda i: (0, 0))],
        out_specs=[pl.BlockSpec((tr, LANES), lambda i: (i, 0)),
                   pl.BlockSpec((tr, LANES), lambda i: (i, 0))],
        out_shape=[shp, shp],
        compiler_params=_params(("parallel",)),
        name="rope_tables",
    )(positions.reshape(t, 1), freq)


def _rope(u, cmul, smul):
    half = MLA_ROPE_DIM // 2
    return u * cmul + (pltpu.roll(u, LANES - half, 1) + pltpu.roll(u, half, 1)) * smul


def _kv_body(h_ref, wd_ref, lg_ref, wu_ref, kgn_ref, kgr_ref, c_ref, s_ref, k_ref, v_ref):
    low = _dot(h_ref[...], wd_ref[...])
    ckv = _rms(low[:, :KV_LORA_RANK], lg_ref[...]).astype(BF16)
    kv = _dot(ckv, wu_ref[...])
    pe = low[:, KV_LORA_RANK:]
    pe_ss = jnp.sum(pe * pe, axis=-1, keepdims=True)
    pe_rot = _rope(pe * kgr_ref[...], c_ref[...], s_ref[...])
    kgn = kgn_ref[...]
    for h in range(MLA_HEADS):
        kn = kv[:, h * HEAD_PAD:h * HEAD_PAD + MLA_NOPE_DIM]
        ms = (jnp.sum(kn * kn, axis=-1, keepdims=True) + pe_ss) * (1.0 / MLA_QK_DIM)
        rinv = lax.rsqrt(ms + EPS)
        k_ref[0, h, :, 0:MLA_NOPE_DIM] = ((kn * rinv) * kgn).astype(k_ref.dtype)
        k_ref[0, h, :, MLA_NOPE_DIM:HEAD_PAD] = (pe_rot * rinv).astype(k_ref.dtype)
        v_ref[0, h, :, 0:MLA_V_DIM] = kv[:, h * HEAD_PAD + MLA_NOPE_DIM:(h + 1) * HEAD_PAD].astype(v_ref.dtype)
        v_ref[0, h, :, MLA_V_DIM:V_EXT] = jnp.ones((kv.shape[0], V_EXT - MLA_V_DIM), v_ref.dtype)


def _shared_kv(h, w_dkv_pad, latent_gain, w_ukv, kg_nope, kg_rope_pad, cmul, smul, bsz):
    t, d = h.shape
    seq = t // bsz
    tps = seq // TM
    return pl.pallas_call(
        _kv_body,
        grid=(t // TM,),
        in_specs=[pl.BlockSpec((TM, d), lambda i: (i, 0)),
                  pl.BlockSpec((d, KV_DOWN_PAD), lambda i: (0, 0)),
                  pl.BlockSpec((1, KV_LORA_RANK), lambda i: (0, 0)),
                  pl.BlockSpec((KV_LORA_RANK, MLA_HEADS * HEAD_PAD), lambda i: (0, 0)),
                  pl.BlockSpec((1, MLA_NOPE_DIM), lambda i: (0, 0)),
                  pl.BlockSpec((1, LANES), lambda i: (0, 0)),
                  pl.BlockSpec((TM, LANES), lambda i: (i, 0)),
                  pl.BlockSpec((TM, LANES), lambda i: (i, 0))],
        out_specs=[pl.BlockSpec((1, MLA_HEADS, TM, HEAD_PAD), lambda i: (i // tps, 0, i % tps, 0)),
                   pl.BlockSpec((1, MLA_HEADS, TM, V_EXT), lambda i: (i // tps, 0, i % tps, 0))],
        out_shape=[jax.ShapeDtypeStruct((bsz, MLA_HEADS, seq, HEAD_PAD), BF16),
                   jax.ShapeDtypeStruct((bsz, MLA_HEADS, seq, V_EXT), BF16)],
        compiler_params=_params(("parallel",)),
        name="shared_kv",
    )(h, w_dkv_pad, latent_gain, w_ukv, kg_nope, kg_rope_pad, cmul, smul)


def _q_body(h_ref, wb_ref, lg_ref, wu_ref, qg_ref, c_ref, s_ref, mq_ref, q_ref, qa_scr, qb_scr):
    s = pl.program_id(0)
    c = (MLA_QK_DIM ** -0.5) * LOG2_E

    @pl.when(s == 0)
    def _():
        qb_scr[...] = jnp.zeros(qb_scr.shape, F32)

    def step(q_new, q_old):
        low = _dot(h_ref[...], wb_ref[...])
        mq_ref[...] = low[:, Q_LORA_PAD:]
        cq = _rms(low[:, :Q_LORA_PAD], lg_ref[...], n=Q_LORA_RANK).astype(BF16)
        q_new[...] = _dot(cq, wu_ref[...])
        qg = qg_ref[...]
        cmul = c_ref[...]
        smul = s_ref[...]
        for h in range(MLA_HEADS):
            qh = _rms(q_old[:, h * HEAD_PAD:(h + 1) * HEAD_PAD], qg, n=MLA_QK_DIM)
            q_ref[0, h, :, 0:MLA_NOPE_DIM] = (qh[:, :MLA_NOPE_DIM] * c).astype(q_ref.dtype)
            q_ref[0, h, :, MLA_NOPE_DIM:HEAD_PAD] = (_rope(qh[:, MLA_NOPE_DIM:], cmul, smul) * c).astype(q_ref.dtype)

    @pl.when(s % 2 == 0)
    def _():
        step(qa_scr, qb_scr)

    @pl.when(s % 2 == 1)
    def _():
        step(qb_scr, qa_scr)


def _mla_q(h, w_b_pad, latent_gain_pad, w_uq_pad, q_gain_pad, cmul, smul, bsz):
    t, d = h.shape
    seq = t // bsz
    tps = seq // TM
    nt = t // TM
    cur = lambda s: jnp.minimum(s, nt - 1)
    prev = lambda s: jnp.maximum(s - 1, 0)
    return pl.pallas_call(
        _q_body,
        grid=(nt + 1,),
        in_specs=[pl.BlockSpec((TM, d), lambda s: (cur(s), 0)),
                  _resident((d, Q_LORA_PAD + MEM_WIDTH)),
                  _resident((1, Q_LORA_PAD)),
                  _resident((Q_LORA_PAD, MLA_HEADS * HEAD_PAD)),
                  _resident((1, HEAD_PAD)),
                  pl.BlockSpec((TM, LANES), lambda s: (prev(s), 0)),
                  pl.BlockSpec((TM, LANES), lambda s: (prev(s), 0))],
        out_specs=[pl.BlockSpec((TM, MEM_WIDTH), lambda s: (cur(s), 0)),
                   pl.BlockSpec((1, MLA_HEADS, TM, HEAD_PAD), lambda s: (prev(s) // tps, 0, prev(s) % tps, 0))],
        out_shape=[jax.ShapeDtypeStruct((t, MEM_WIDTH), F32),
                   jax.ShapeDtypeStruct((bsz, MLA_HEADS, seq, HEAD_PAD), BF16)],
        scratch_shapes=[pltpu.VMEM((TM, MLA_HEADS * HEAD_PAD), F32),
                        pltpu.VMEM((TM, MLA_HEADS * HEAD_PAD), F32)],
        compiler_params=_params(("arbitrary",)),
        name="proj_mla_q",
    )(h, w_b_pad, latent_gain_pad, w_uq_pad, q_gain_pad, cmul, smul)


def _flash_body(qi_tab, ki_tab, last_tab, q_ref, k_ref, v_ref, o_ref, m_scr, acc_scr):
    p_id = pl.program_id(1)
    diag = qi_tab[p_id] == ki_tab[p_id]
    reps = TQ // LANES

    def scores(h):
        return _dot_nt(q_ref[0, h], k_ref[0, h])

    @pl.when(diag)
    def _():
        r_i = lax.broadcasted_iota(jnp.int32, (TQ, TQ), 0)
        c_i = lax.broadcasted_iota(jnp.int32, (TQ, TQ), 1)
        keep = c_i <= r_i
        for h in range(MLA_HEADS):
            s = jnp.where(keep, scores(h), -jnp.inf)
            m = jnp.broadcast_to(jnp.max(s, axis=-1, keepdims=True), (TQ, LANES))
            e = jnp.exp2(s - _lane_tile(m, reps))
            m_scr[h] = m
            acc_scr[h] = _dot(e.astype(BF16), v_ref[0, h])

    @pl.when(jnp.logical_not(diag))
    def _():
        for h in range(MLA_HEADS):
            s = scores(h)
            m_prev = m_scr[h]
            m = jnp.maximum(m_prev, jnp.max(s, axis=-1, keepdims=True))
            alpha = jnp.exp2(m_prev - m)
            e = jnp.exp2(s - _lane_tile(m, reps))
            m_scr[h] = m
            acc_scr[h] = _lane_tile(alpha, V_EXT // LANES) * acc_scr[h] + _dot(e.astype(BF16), v_ref[0, h])

    @pl.when(last_tab[p_id] == 1)
    def _():
        for h in range(MLA_HEADS):
            acc = acc_scr[h]
            o_ref[:, h * MLA_V_DIM:(h + 1) * MLA_V_DIM] = (
                acc[:, :MLA_V_DIM] * (1.0 / acc[:, MLA_V_DIM:])).astype(o_ref.dtype)


def _flash(q, k, v):
    bsz, nh, seq, _ = q.shape
    nq = seq // TQ
    qi, ki, last = [], [], []
    for a in range(nq):
        order = [a] + list(range(a))
        for n, b in enumerate(order):
            qi.append(a)
            ki.append(b)
            last.append(1 if n == len(order) - 1 else 0)
    npairs = len(qi)
    tabs = [jnp.asarray(x, jnp.int32) for x in (qi, ki, last)]
    grid_spec = pltpu.PrefetchScalarGridSpec(
        num_scalar_prefetch=3,
        grid=(bsz, npairs),
        in_specs=[pl.BlockSpec((1, nh, TQ, HEAD_PAD), lambda b, p, qt, kt, lt: (b, 0, qt[p], 0)),
                  pl.BlockSpec((1, nh, TQ, HEAD_PAD), lambda b, p, qt, kt, lt: (b, 0, kt[p], 0)),
                  pl.BlockSpec((1, nh, TQ, V_EXT), lambda b, p, qt, kt, lt: (b, 0, kt[p], 0))],
        out_specs=pl.BlockSpec((TQ, nh * MLA_V_DIM), lambda b, p, qt, kt, lt: (b * nq + qt[p], 0)),
        scratch_shapes=[pltpu.VMEM((nh, TQ, LANES), F32),
                        pltpu.VMEM((nh, TQ, V_EXT), F32)],
    )
    return pl.pallas_call(
        _flash_body,
        grid_spec=grid_spec,
        out_shape=jax.ShapeDtypeStruct((bsz * seq, nh * MLA_V_DIM), BF16),
        compiler_params=_params(("parallel", "arbitrary")),
        name="mla_attention",
    )(*tabs, q, k, v)


def _pad_heads(w, nh, dh, dh_pad):
    lead = w.shape[:-1]
    w = w.reshape(lead + (nh, dh))
    w = jnp.pad(w, [(0, 0)] * len(lead) + [(0, 0), (0, dh_pad - dh)])
    return w.reshape(lead + (nh * dh_pad,))


def _pad_last(w, n):
    return jnp.pad(w, [(0, 0)] * (w.ndim - 1) + [(0, n - w.shape[-1])])


def kernel(x, mem, positions, ffn1_gain, ffn1_w_in, ffn1_w_out, mix_gain, w_out, mem_gain, w_mem_kv, mem_q_gain, mem_k_gain, a_w_in, a_b_gates, a_conv, a_head_gain, kv_gain, w_dkv, kv_latent_gain, w_ukv, k_gain, b_w_in, b_q_latent_gain, b_w_uq, b_q_gain, ffn2_gain, ffn2_w_in, ffn2_w_out):
    bsz, seq, d = x.shape
    t = bsz * seq
    assert d == D_MODEL and seq % TM == 0 and seq % CHUNK == 0 and seq % TQ == 0 and t % min(TROPE, t) == 0 and t % TM_FFN == 0

    ffn_w = (ffn1_w_in[0].astype(BF16), ffn1_w_out[0].astype(BF16))
    w_out_b = w_out.astype(BF16)

    nqk = MLSTM_HEADS * MLSTM_QK_DIM
    c0, c1, c2, c3 = 2 * nqk, 2 * nqk + MAIN_WIDTH, 2 * nqk + 2 * MAIN_WIDTH, 2 * nqk + 2 * MAIN_WIDTH + 2 * MLSTM_HEADS

    cmul, smul = _rope_tables(positions)
    mk, mv = _mem_kv(mem.reshape(bsz * N_MEM, d), mem_gain, w_mem_kv.astype(BF16), mem_k_gain)

    xs = x.reshape(t, d)
    k_sh = v_sh = None
    for layer in range(DEPTH):
        xs, hmix, ffn_w = _ffn(xs, ffn1_gain[layer], *ffn_w, next_gain=mix_gain[layer],
                               next_weights=(ffn2_w_in, ffn2_w_out, layer))
        if layer < N_A_LAYERS:
            w_in = a_w_in[layer]
            w_qk = jnp.concatenate([_pad_heads(w_in[:, :nqk], MLSTM_HEADS, MLSTM_QK_DIM, QK_PAD),
                                    _pad_heads(w_in[:, nqk:c0], MLSTM_HEADS, MLSTM_QK_DIM, QK_PAD)], axis=-1)
            conv_w = jnp.concatenate([_pad_heads(a_conv[layer][:, :nqk], MLSTM_HEADS, MLSTM_QK_DIM, QK_PAD),
                                      _pad_heads(a_conv[layer][:, nqk:], MLSTM_HEADS, MLSTM_QK_DIM, QK_PAD)], axis=-1)
            w_g = w_in[:, c2:c3]
            qk, v, og, mq, g, gt = _proj_a(
                hmix, w_qk.astype(BF16), w_in[:, c0:c1].astype(BF16), w_in[:, c1:c2].astype(BF16),
                w_in[:, c3:].astype(BF16), _pad_last(w_g, GATE_PAD).astype(BF16),
                jnp.pad(w_g.T, ((0, BF16_ROWS - 2 * MLSTM_HEADS), (0, 0))).astype(BF16), a_b_gates[layer])
            main = _mlstm(qk, conv_w, v, og, g, gt, a_head_gain[layer], bsz)
        else:
            j = layer - N_A_LAYERS
            w_in = b_w_in[j]
            w_b = jnp.concatenate([_pad_last(w_in[:, :Q_LORA_RANK], Q_LORA_PAD), w_in[:, Q_LORA_RANK:]], axis=-1)
            w_uq = jnp.pad(_pad_heads(b_w_uq[j], MLA_HEADS, MLA_QK_DIM, HEAD_PAD),
                           ((0, Q_LORA_PAD - Q_LORA_RANK), (0, 0))).astype(BF16)
            mq, q = _mla_q(hmix, w_b.astype(BF16), _pad_last(b_q_latent_gain[j], Q_LORA_PAD).reshape(1, Q_LORA_PAD),
                           w_uq, _pad_last(b_q_gain[j], HEAD_PAD).reshape(1, HEAD_PAD), cmul, smul, bsz)
            main = _flash(q, k_sh, v_sh)
        xs = _mixer_out(xs, main, mq, mk, mv, mem_q_gain.reshape(DEPTH, 1, MEM_HEAD_DIM), w_out_b, layer, seq)
        following = (ffn1_w_in, ffn1_w_out, layer + 1) if layer + 1 < DEPTH else None
        xs, hkv, ffn_w = _ffn(xs, ffn2_gain[layer], *ffn_w, next_gain=kv_gain if layer == N_A_LAYERS - 1 else None,
                              next_weights=following)
        if layer == N_A_LAYERS - 1:
            k_sh, v_sh = _shared_kv(hkv, _pad_last(w_dkv, KV_DOWN_PAD).astype(BF16),
                                    kv_latent_gain.reshape(1, KV_LORA_RANK), w_ukv.astype(BF16),
                                    k_gain[:MLA_NOPE_DIM].reshape(1, MLA_NOPE_DIM),
                                    _pad_last(k_gain[MLA_NOPE_DIM:], LANES).reshape(1, LANES), cmul, smul, bsz)
    return xs.reshape(bsz, seq, d)
```

```python
import functools

import jax
import jax.numpy as jnp
from jax import lax
from jax.experimental import pallas as pl
from jax.experimental.pallas import tpu as pltpu

F32 = jnp.float32
BF16 = jnp.bfloat16

D_MODEL = 2048
DEPTH = 4
N_MEM = 256
N_A_LAYERS = DEPTH // 2
EPS = 1e-6
MEM_HEADS = 4
MEM_WIDTH = D_MODEL // 4
MEM_HEAD_DIM = MEM_WIDTH // MEM_HEADS
MAIN_WIDTH = D_MODEL - MEM_WIDTH
MLSTM_HEADS = 4
MLSTM_V_DIM = MAIN_WIDTH // MLSTM_HEADS
MLSTM_QK_DIM = MLSTM_V_DIM // 2
CONV_WIDTH = 4
MLA_NOPE_DIM = 128
MLA_ROPE_DIM = 64
MLA_QK_DIM = MLA_NOPE_DIM + MLA_ROPE_DIM
MLA_V_DIM = 128
MLA_HEADS = MAIN_WIDTH // MLA_V_DIM
Q_LORA_RANK = 448
KV_LORA_RANK = 512
ROPE_THETA = 10000.0
D_FF = 5632
LOG2_E = 1.4426950408889634

LANES = 128
SUBLANES = 8
BF16_ROWS = 2 * SUBLANES
V7X_VMEM_BYTES = 64 * 1024 * 1024

QK_PAD = 256
Q_LORA_PAD = 512
HEAD_PAD = 256
GATE_PAD = LANES
KV_DOWN_PAD = KV_LORA_RANK + LANES
V_EXT = MLA_V_DIM + LANES

TM = 512
TM_FFN = 1024
TF = 512
CHUNK = 256
TQ = 512
TROPE = 2048
MIB = 1024 * 1024
VMEM_LIMIT = 56 * MIB
VMEM_LIMIT_FFN = 58 * MIB
assert VMEM_LIMIT_FFN < V7X_VMEM_BYTES


def _params(sem, vmem_limit=VMEM_LIMIT):
    return pltpu.CompilerParams(dimension_semantics=sem, vmem_limit_bytes=vmem_limit)


def _rms(x, gain, n=None):
    if n is None:
        ms = jnp.mean(x * x, axis=-1, keepdims=True)
    else:
        ms = jnp.sum(x * x, axis=-1, keepdims=True) * (1.0 / n)
    return (x * lax.rsqrt(ms + EPS)) * gain


def _log_sigmoid(x):
    return -(jnp.maximum(-x, 0.0) + jnp.log1p(jnp.exp(-jnp.abs(x))))


def _lane_tile(x, reps):
    return jnp.concatenate([x] * reps, axis=1)


def _dot(a, b):
    return jnp.dot(a, b, preferred_element_type=F32)


def _dot_nt(a, b):
    return lax.dot_general(a, b, (((1,), (1,)), ((), ())), preferred_element_type=F32)


def _dot_tn(a, b):
    return lax.dot_general(a, b, (((0,), (0,)), ((), ())), preferred_element_type=F32)


def _ffn_body(emit_next, cast_next, x_hbm, g_ref, wg_ref, wu_ref, wo_ref, *rest):
    rest = list(rest)
    ng_ref = rest.pop(0) if emit_next else None
    nwi_ref, nwo_ref = (rest.pop(0), rest.pop(0)) if cast_next else (None, None)
    o_ref = rest.pop(0)
    hn_ref = rest.pop(0) if emit_next else None
    nwi_out, nwo_out = (rest.pop(0), rest.pop(0)) if cast_next else (None, None)
    h_scr, x_buf, x_sem = rest
    i = pl.program_id(0)
    j = pl.program_id(1)

    def x_copy(tile):
        rows = pl.ds(pl.multiple_of(tile * TM_FFN, TM_FFN), TM_FFN)
        return pltpu.make_async_copy(x_hbm.at[rows, :], x_buf, x_sem)

    def swiglu_chunk():
        if cast_next:
            nwi_out[...] = nwi_ref[...].astype(BF16)
            nwo_out[...] = nwo_ref[...].astype(BF16)
        h = h_scr[...]
        g = _dot(h, wg_ref[...])
        u = _dot(h, wu_ref[...])
        a = ((g * jax.nn.sigmoid(g)) * u) * 0.5
        return _dot(a.astype(BF16), wo_ref[...])

    @pl.when(j == 0)
    def _():
        @pl.when(i == 0)
        def _():
            x_copy(0).start()

        x_copy(i).wait()
        h_scr[...] = _rms(x_buf[...], g_ref[...]).astype(BF16)
        o_ref[...] = x_buf[...] + swiglu_chunk()

    @pl.when(jnp.logical_and(j == 1, i + 1 < pl.num_programs(0)))
    def _():
        x_copy(i + 1).start()

    @pl.when(j > 0)
    def _():
        o_ref[...] += swiglu_chunk()

    if emit_next:
        @pl.when(j == pl.num_programs(1) - 1)
        def _():
            hn_ref[...] = _rms(o_ref[...], ng_ref[...]).astype(BF16)


def _ffn(x, gain, w_in, w_out, next_gain=None, next_weights=None):
    t, d = x.shape
    nf = D_FF // TF
    ni = t // TM_FFN
    assert nf >= 2 and t % TM_FFN == 0
    emit = next_gain is not None
    cast = next_weights is not None
    in_specs = [
        pl.BlockSpec(memory_space=pl.ANY),
        pl.BlockSpec((1, d), lambda i, j: (0, 0)),
        pl.BlockSpec((d, TF), lambda i, j: (0, j)),
        pl.BlockSpec((d, TF), lambda i, j: (0, j + nf)),
        pl.BlockSpec((TF, d), lambda i, j: (j, 0)),
    ]
    args = [x, gain.reshape(1, d), w_in, w_in, w_out]
    out_shape = [jax.ShapeDtypeStruct((t, d), F32)]
    out_specs = [pl.BlockSpec((TM_FFN, d), lambda i, j: (i, 0))]
    if emit:
        in_specs.append(pl.BlockSpec((1, d), lambda i, j: (0, 0)))
        args.append(next_gain.reshape(1, d))
        out_shape.append(jax.ShapeDtypeStruct((t, d), BF16))
        out_specs.append(pl.BlockSpec((TM_FFN, d), lambda i, j: (i, 0)))
    if cast:
        nwi, nwo, nl = next_weights
        ri, ci = d // ni, 2 * D_FF // nf
        ro = D_FF // (ni * nf)
        assert d % ni == 0 and ri % BF16_ROWS == 0 and ci % LANES == 0
        assert D_FF % (ni * nf) == 0 and ro % BF16_ROWS == 0
        in_specs += [pl.BlockSpec((None, ri, ci), lambda i, j: (nl, i, j)),
                     pl.BlockSpec((None, ro, d), lambda i, j: (nl, i * nf + j, 0))]
        args += [nwi, nwo]
        out_shape += [jax.ShapeDtypeStruct((d, 2 * D_FF), BF16), jax.ShapeDtypeStruct((D_FF, d), BF16)]
        out_specs += [pl.BlockSpec((ri, ci), lambda i, j: (i, j)),
                      pl.BlockSpec((ro, d), lambda i, j: (i * nf + j, 0))]
    res = pl.pallas_call(
        functools.partial(_ffn_body, emit, cast),
        grid=(ni, nf),
        in_specs=in_specs,
        out_specs=out_specs,
        out_shape=out_shape,
        scratch_shapes=[pltpu.VMEM((TM_FFN, d), BF16),
                        pltpu.VMEM((TM_FFN, d), F32),
                        pltpu.SemaphoreType.DMA(())],
        compiler_params=_params(("arbitrary", "arbitrary"), VMEM_LIMIT_FFN),
        name="ffn" + ("_next" if emit else "") + ("_cast" if cast else ""),
    )(*args)
    x_new = res[0]
    hn = res[1] if emit else None
    next_bf16 = (res[-2], res[-1]) if cast else None
    return x_new, hn, next_bf16


def _resident(shape):
    return pl.BlockSpec(shape, lambda i: (0,) * len(shape), pipeline_mode=pl.Buffered(1))


def _proja_body(h_ref, wqk_ref, wv_ref, wo_ref, wmq_ref, wg_ref, wgt_ref, brow_ref, bcol_ref,
                qk_ref, v_ref, og_ref, mq_ref, g_ref, gt_ref):
    h = h_ref[...]
    qk_ref[...] = _dot(h, wqk_ref[...])
    v_ref[...] = _dot(h, wv_ref[...]).astype(v_ref.dtype)
    og_ref[...] = _dot(h, wo_ref[...])
    mq_ref[...] = _dot(h, wmq_ref[...])
    g = _dot(h, wg_ref[...]) + brow_ref[...]
    lane = lax.broadcasted_iota(jnp.int32, g.shape, 1)
    g_ref[...] = jnp.where(lane >= MLSTM_HEADS, _log_sigmoid(g), g)
    gt = _dot_nt(wgt_ref[...], h)[:2 * MLSTM_HEADS] + bcol_ref[...]
    row = lax.broadcasted_iota(jnp.int32, gt.shape, 0)
    gt_ref[...] = jnp.where(row >= MLSTM_HEADS, _log_sigmoid(gt), gt)


def _proj_a(h, w_qk, w_v, w_o, w_mq, w_g, w_gt, bias):
    t, d = h.shape
    ng = 2 * MLSTM_HEADS
    nqk = w_qk.shape[1]
    brow = jnp.zeros((1, GATE_PAD), F32).at[0, :ng].set(bias)
    bcol = bias.reshape(ng, 1)
    row_spec = lambda n: pl.BlockSpec((TM, n), lambda i: (i, 0))
    return pl.pallas_call(
        _proja_body,
        grid=(t // TM,),
        in_specs=[row_spec(d), _resident((d, nqk)), _resident((d, MAIN_WIDTH)), _resident((d, MAIN_WIDTH)),
                  _resident((d, MEM_WIDTH)), _resident((d, GATE_PAD)), _resident((BF16_ROWS, d)),
                  _resident((1, GATE_PAD)), _resident((ng, 1))],
        out_specs=[row_spec(nqk), row_spec(MAIN_WIDTH), row_spec(MAIN_WIDTH), row_spec(MEM_WIDTH),
                   row_spec(GATE_PAD), pl.BlockSpec((ng, TM), lambda i: (0, i))],
        out_shape=[jax.ShapeDtypeStruct((t, nqk), F32),
                   jax.ShapeDtypeStruct((t, MAIN_WIDTH), BF16),
                   jax.ShapeDtypeStruct((t, MAIN_WIDTH), F32),
                   jax.ShapeDtypeStruct((t, MEM_WIDTH), F32),
                   jax.ShapeDtypeStruct((t, GATE_PAD), F32),
                   jax.ShapeDtypeStruct((ng, t), F32)],
        compiler_params=_params(("parallel",)),
        name="proj_mlstm",
    )(h, w_qk, w_v, w_o, w_mq, w_g, w_gt, brow, bcol)


def _mlstm_body(qk_ref, cw_ref, v_ref, og_ref, g_ref, gt_ref, hg_ref, out_ref, xbuf, c_scr, m_scr):
    nh, dkp, dv = MLSTM_HEADS, QK_PAD, MLSTM_V_DIM
    ell = CHUNK
    halo = SUBLANES

    @pl.when(pl.program_id(1) == 0)
    def _():
        xbuf[0:halo, :] = jnp.zeros((halo, xbuf.shape[1]), F32)
        c_scr[...] = jnp.zeros(c_scr.shape, F32)
        m_scr[...] = jnp.zeros(m_scr.shape, F32)

    xbuf[halo:halo + ell, :] = qk_ref[...]
    cw = cw_ref[...]
    y = cw[CONV_WIDTH - 1:CONV_WIDTH, :] * xbuf[halo:halo + ell, :]
    for j in range(CONV_WIDTH - 1):
        sh = CONV_WIDTH - 1 - j
        y = y + cw[j:j + 1, :] * xbuf[halo - sh:halo - sh + ell, :]
    xbuf[0:halo, :] = xbuf[ell:ell + halo, :]
    y = y * jax.nn.sigmoid(y)

    r_i = lax.broadcasted_iota(jnp.int32, (ell, ell), 0)
    c_i = lax.broadcasted_iota(jnp.int32, (ell, ell), 1)
    causal = c_i <= r_i
    tri = jnp.where(causal, 1.0, 0.0).astype(F32)
    tri_t = jnp.where(r_i <= c_i, 1.0, 0.0).astype(F32)
    gcol = g_ref[...]
    grow = gt_ref[...]
    bcol_all = jnp.dot(tri, gcol, preferred_element_type=F32, precision=lax.Precision.HIGHEST)
    brow_all = jnp.dot(grow, tri_t, preferred_element_type=F32, precision=lax.Precision.HIGHEST)

    lane = lax.broadcasted_iota(jnp.int32, (ell, LANES), 1)
    ones_col = jnp.where(lane == 0, 1.0, 0.0).astype(BF16)
    v_all = v_ref[...]
    og = og_ref[...]
    hg = hg_ref[...]

    for h in range(nh):
        q = (y[:, h * dkp:(h + 1) * dkp] * (MLSTM_QK_DIM ** -0.5)).astype(BF16)
        k_f = y[:, (nh + h) * dkp:(nh + h + 1) * dkp]
        k = k_f.astype(BF16)
        v_ext = jnp.concatenate([v_all[:, h * dv:(h + 1) * dv], ones_col], axis=-1)
        ig_col = gcol[:, h:h + 1]
        ig_row = grow[h:h + 1, :]
        b_col = bcol_all[:, nh + h:nh + h + 1]
        b_row = brow_all[nh + h:nh + h + 1, :]
        m_prev = m_scr[h][:, 0:1]
        c_prev = c_scr[h]

        log_w = jnp.where(causal, b_col - b_row + ig_row, -jnp.inf)
        log_inter = b_col + m_prev
        m_t = jnp.maximum(jnp.max(log_w, axis=-1, keepdims=True), log_inter)
        w = jnp.exp(log_w - m_t)
        w_inter = jnp.exp(log_inter - m_t)
        s = _dot_nt(q, k) * w
        inter = _dot(q, c_prev.astype(BF16))
        num = _dot(s.astype(BF16), v_ext[:, :dv]) + w_inter * inter[:, :dv]
        den = jnp.sum(s, axis=-1, keepdims=True) + w_inter * inter[:, dv:dv + 1]
        hh = num * (1.0 / jnp.maximum(jnp.abs(den), jnp.exp(-m_t)))

        g_last = b_col[ell - 1:ell, :]
        log_a = g_last - b_col + ig_col
        m_new = jnp.maximum(g_last + m_prev, jnp.max(log_a, axis=0, keepdims=True))
        decay = jnp.exp(g_last + m_prev - m_new)
        a = jnp.exp(log_a - m_new)
        c_scr[h] = decay * c_prev + _dot_tn((k_f * a).astype(BF16), v_ext)
        m_scr[h] = jnp.broadcast_to(m_new, (1, LANES))

        hn = _rms(hh, hg[:, h * dv:(h + 1) * dv])
        o_h = og[:, h * dv:(h + 1) * dv]
        out_ref[:, h * dv:(h + 1) * dv] = (jax.nn.sigmoid(o_h) * hn).astype(out_ref.dtype)


def _mlstm(qk, conv_w, v, og, g, gt, head_gain, bsz):
    t = qk.shape[0]
    nc = t // bsz // CHUNK
    nqk = qk.shape[1]
    return pl.pallas_call(
        _mlstm_body,
        grid=(bsz, nc),
        in_specs=[pl.BlockSpec((CHUNK, nqk), lambda b, c: (b * nc + c, 0)),
                  pl.BlockSpec((CONV_WIDTH, nqk), lambda b, c: (0, 0)),
                  pl.BlockSpec((CHUNK, MAIN_WIDTH), lambda b, c: (b * nc + c, 0)),
                  pl.BlockSpec((CHUNK, MAIN_WIDTH), lambda b, c: (b * nc + c, 0)),
                  pl.BlockSpec((CHUNK, GATE_PAD), lambda b, c: (b * nc + c, 0)),
                  pl.BlockSpec((2 * MLSTM_HEADS, CHUNK), lambda b, c: (0, b * nc + c)),
                  pl.BlockSpec((1, MAIN_WIDTH), lambda b, c: (0, 0))],
        out_specs=pl.BlockSpec((CHUNK, MAIN_WIDTH), lambda b, c: (b * nc + c, 0)),
        out_shape=jax.ShapeDtypeStruct((t, MAIN_WIDTH), BF16),
        scratch_shapes=[pltpu.VMEM((CHUNK + SUBLANES, nqk), F32),
                        pltpu.VMEM((MLSTM_HEADS, QK_PAD, MLSTM_V_DIM + LANES), F32),
                        pltpu.VMEM((MLSTM_HEADS, 1, LANES), F32)],
        compiler_params=_params(("parallel", "arbitrary")),
        name="mlstm",
    )(qk, conv_w, v, og, g, gt, head_gain.reshape(1, MAIN_WIDTH))


def _memkv_body(mem_ref, g_ref, w_ref, kg_ref, mk_ref, mv_ref):
    hn = _rms(mem_ref[...], g_ref[...]).astype(BF16)
    kv = _dot(hn, w_ref[...])
    kg = kg_ref[...]
    for h in range(MEM_HEADS):
        sl = slice(h * MEM_HEAD_DIM, (h + 1) * MEM_HEAD_DIM)
        mk_ref[:, sl] = _rms(kv[:, sl], kg).astype(mk_ref.dtype)
    mv_ref[...] = kv[:, MEM_WIDTH:].astype(mv_ref.dtype)


def _mem_kv(mem2d, mem_gain, w_mem_kv, mem_k_gain):
    rows = mem2d.shape[0]
    shp = jax.ShapeDtypeStruct((DEPTH, rows, MEM_WIDTH), BF16)
    return pl.pallas_call(
        _memkv_body,
        grid=(DEPTH,),
        in_specs=[pl.BlockSpec((rows, D_MODEL), lambda l: (0, 0)),
                  pl.BlockSpec((None, 1, D_MODEL), lambda l: (l, 0, 0)),
                  pl.BlockSpec((None, D_MODEL, 2 * MEM_WIDTH), lambda l: (l, 0, 0)),
                  pl.BlockSpec((None, 1, MEM_HEAD_DIM), lambda l: (l, 0, 0))],
        out_specs=[pl.BlockSpec((None, rows, MEM_WIDTH), lambda l: (l, 0, 0)),
                   pl.BlockSpec((None, rows, MEM_WIDTH), lambda l: (l, 0, 0))],
        out_shape=[shp, shp],
        compiler_params=_params(("parallel",)),
        name="mem_kv",
    )(mem2d, mem_gain.reshape(DEPTH, 1, D_MODEL), w_mem_kv, mem_k_gain.reshape(DEPTH, 1, MEM_HEAD_DIM))


def _mix_body(x_ref, main_ref, mq_ref, mk_ref, mv_ref, qg_ref, wmain_ref, wmem_ref, o_ref):
    mq = mq_ref[...]
    mk = mk_ref[...]
    mv = mv_ref[...]
    qg = qg_ref[...]
    heads = []
    for h in range(MEM_HEADS):
        sl = slice(h * MEM_HEAD_DIM, (h + 1) * MEM_HEAD_DIM)
        qn = _rms(mq[:, sl], qg).astype(BF16)
        s = _dot_nt(qn, mk[:, sl]) * (MEM_HEAD_DIM ** -0.5)
        e = jnp.exp(s - jnp.max(s, axis=-1, keepdims=True))
        oh = _dot(e.astype(BF16), mv[:, sl]) * (1.0 / jnp.sum(e, axis=-1, keepdims=True))
        heads.append(oh.astype(BF16))
    mem_out = jnp.concatenate(heads, axis=-1)
    o_ref[...] = x_ref[...] + _dot(main_ref[...], wmain_ref[...]) + _dot(mem_out, wmem_ref[...])


def _mixer_out(x, main, mq, mk, mv, q_gain, w_out, layer, seq):
    t, d = x.shape
    tiles_per_seq = seq // TM
    return pl.pallas_call(
        _mix_body,
        grid=(t // TM,),
        in_specs=[pl.BlockSpec((TM, d), lambda i: (i, 0)),
                  pl.BlockSpec((TM, MAIN_WIDTH), lambda i: (i, 0)),
                  pl.BlockSpec((TM, MEM_WIDTH), lambda i: (i, 0)),
                  pl.BlockSpec((None, N_MEM, MEM_WIDTH), lambda i: (layer, i // tiles_per_seq, 0)),
                  pl.BlockSpec((None, N_MEM, MEM_WIDTH), lambda i: (layer, i // tiles_per_seq, 0)),
                  pl.BlockSpec((None, 1, MEM_HEAD_DIM), lambda i: (layer, 0, 0)),
                  pl.BlockSpec((None, MAIN_WIDTH, d), lambda i: (layer, 0, 0)),
                  pl.BlockSpec((None, MEM_WIDTH, d), lambda i: (layer, MAIN_WIDTH // MEM_WIDTH, 0))],
        out_specs=pl.BlockSpec((TM, d), lambda i: (i, 0)),
        out_shape=jax.ShapeDtypeStruct((t, d), F32),
        compiler_params=_params(("parallel",)),
        name="mixer_out",
    )(x, main, mq, mk, mv, q_gain, w_out, w_out)


def _rope_body(pos_ref, freq_ref, c_ref, s_ref):
    ang = pos_ref[...].astype(F32) * freq_ref[...]
    lane = lax.broadcasted_iota(jnp.int32, ang.shape, 1)
    half = MLA_ROPE_DIM // 2
    cos = jnp.cos(ang)
    sin = jnp.sin(ang)
    c_ref[...] = jnp.where(lane < MLA_ROPE_DIM, cos, 0.0)
    s_ref[...] = jnp.where(lane < half, -sin, jnp.where(lane < MLA_ROPE_DIM, sin, 0.0))


def _rope_tables(positions):
    t = positions.size
    tr = min(TROPE, t)
    half = MLA_ROPE_DIM // 2
    inv_freq = ROPE_THETA ** (-jnp.arange(0, MLA_ROPE_DIM, 2, dtype=F32) / MLA_ROPE_DIM)
    freq = jnp.tile(inv_freq, LANES // half).reshape(1, LANES)
    shp = jax.ShapeDtypeStruct((t, LANES), F32)
    return pl.pallas_call(
        _rope_body,
        grid=(t // tr,),
        in_specs=[pl.BlockSpec((tr, 1), lambda i: (i, 0)),
                  pl.BlockSpec((1, LANES), lambda i: (0, 0))],
        out_specs=[pl.BlockSpec((tr, LANES), lambda i: (i, 0)),
                   pl.BlockSpec((tr, LANES), lambda i: (i, 0))],
        out_shape=[shp, shp],
        compiler_params=_params(("parallel",)),
        name="rope_tables",
    )(positions.reshape(t, 1), freq)


def _rope(u, cmul, smul):
    half = MLA_ROPE_DIM // 2
    return u * cmul + (pltpu.roll(u, LANES - half, 1) + pltpu.roll(u, half, 1)) * smul


def _kv_body(h_ref, wd_ref, lg_ref, wu_ref, kgn_ref, kgr_ref, c_ref, s_ref, k_ref, v_ref):
    low = _dot(h_ref[...], wd_ref[...])
    ckv = _rms(low[:, :KV_LORA_RANK], lg_ref[...]).astype(BF16)
    kv = _dot(ckv, wu_ref[...])
    pe = low[:, KV_LORA_RANK:]
    pe_ss = jnp.sum(pe * pe, axis=-1, keepdims=True)
    pe_rot = _rope(pe * kgr_ref[...], c_ref[...], s_ref[...])
    kgn = kgn_ref[...]
    for h in range(MLA_HEADS):
        kn = kv[:, h * HEAD_PAD:h * HEAD_PAD + MLA_NOPE_DIM]
        ms = (jnp.sum(kn * kn, axis=-1, keepdims=True) + pe_ss) * (1.0 / MLA_QK_DIM)
        rinv = lax.rsqrt(ms + EPS)
        k_ref[0, h, :, 0:MLA_NOPE_DIM] = ((kn * rinv) * kgn).astype(k_ref.dtype)
        k_ref[0, h, :, MLA_NOPE_DIM:HEAD_PAD] = (pe_rot * rinv).astype(k_ref.dtype)
        v_ref[0, h, :, 0:MLA_V_DIM] = kv[:, h * HEAD_PAD + MLA_NOPE_DIM:(h + 1) * HEAD_PAD].astype(v_ref.dtype)
        v_ref[0, h, :, MLA_V_DIM:V_EXT] = jnp.ones((kv.shape[0], V_EXT - MLA_V_DIM), v_ref.dtype)


def _shared_kv(h, w_dkv_pad, latent_gain, w_ukv, kg_nope, kg_rope_pad, cmul, smul, bsz):
    t, d = h.shape
    seq = t // bsz
    tps = seq // TM
    return pl.pallas_call(
        _kv_body,
        grid=(t // TM,),
        in_specs=[pl.BlockSpec((TM, d), lambda i: (i, 0)),
                  pl.BlockSpec((d, KV_DOWN_PAD), lambda i: (0, 0)),
                  pl.BlockSpec((1, KV_LORA_RANK), lambda i: (0, 0)),
                  pl.BlockSpec((KV_LORA_RANK, MLA_HEADS * HEAD_PAD), lambda i: (0, 0)),
                  pl.BlockSpec((1, MLA_NOPE_DIM), lambda i: (0, 0)),
                  pl.BlockSpec((1, LANES), lambda i: (0, 0)),
                  pl.BlockSpec((TM, LANES), lambda i: (i, 0)),
                  pl.BlockSpec((TM, LANES), lambda i: (i, 0))],
        out_specs=[pl.BlockSpec((1, MLA_HEADS, TM, HEAD_PAD), lambda i: (i // tps, 0, i % tps, 0)),
                   pl.BlockSpec((1, MLA_HEADS, TM, V_EXT), lambda i: (i // tps, 0, i % tps, 0))],
        out_shape=[jax.ShapeDtypeStruct((bsz, MLA_HEADS, seq, HEAD_PAD), BF16),
                   jax.ShapeDtypeStruct((bsz, MLA_HEADS, seq, V_EXT), BF16)],
        compiler_params=_params(("parallel",)),
        name="shared_kv",
    )(h, w_dkv_pad, latent_gain, w_ukv, kg_nope, kg_rope_pad, cmul, smul)


def _q_body(h_ref, wb_ref, lg_ref, wu_ref, qg_ref, c_ref, s_ref, mq_ref, q_ref, qa_scr, qb_scr):
    s = pl.program_id(0)
    c = (MLA_QK_DIM ** -0.5) * LOG2_E

    @pl.when(s == 0)
    def _():
        qb_scr[...] = jnp.zeros(qb_scr.shape, F32)

    def step(q_new, q_old):
        low = _dot(h_ref[...], wb_ref[...])
        mq_ref[...] = low[:, Q_LORA_PAD:]
        cq = _rms(low[:, :Q_LORA_PAD], lg_ref[...], n=Q_LORA_RANK).astype(BF16)
        q_new[...] = _dot(cq, wu_ref[...])
        qg = qg_ref[...]
        cmul = c_ref[...]
        smul = s_ref[...]
        for h in range(MLA_HEADS):
            qh = _rms(q_old[:, h * HEAD_PAD:(h + 1) * HEAD_PAD], qg, n=MLA_QK_DIM)
            q_ref[0, h, :, 0:MLA_NOPE_DIM] = (qh[:, :MLA_NOPE_DIM] * c).astype(q_ref.dtype)
            q_ref[0, h, :, MLA_NOPE_DIM:HEAD_PAD] = (_rope(qh[:, MLA_NOPE_DIM:], cmul, smul) * c).astype(q_ref.dtype)

    @pl.when(s % 2 == 0)
    def _():
        step(qa_scr, qb_scr)

    @pl.when(s % 2 == 1)
    def _():
        step(qb_scr, qa_scr)


def _mla_q(h, w_b_pad, latent_gain_pad, w_uq_pad, q_gain_pad, cmul, smul, bsz):
    t, d = h.shape
    seq = t // bsz
    tps = seq // TM
    nt = t // TM
    cur = lambda s: jnp.minimum(s, nt - 1)
    prev = lambda s: jnp.maximum(s - 1, 0)
    return pl.pallas_call(
        _q_body,
        grid=(nt + 1,),
        in_specs=[pl.BlockSpec((TM, d), lambda s: (cur(s), 0)),
                  _resident((d, Q_LORA_PAD + MEM_WIDTH)),
                  _resident((1, Q_LORA_PAD)),
                  _resident((Q_LORA_PAD, MLA_HEADS * HEAD_PAD)),
                  _resident((1, HEAD_PAD)),
                  pl.BlockSpec((TM, LANES), lambda s: (prev(s), 0)),
                  pl.BlockSpec((TM, LANES), lambda s: (prev(s), 0))],
        out_specs=[pl.BlockSpec((TM, MEM_WIDTH), lambda s: (cur(s), 0)),
                   pl.BlockSpec((1, MLA_HEADS, TM, HEAD_PAD), lambda s: (prev(s) // tps, 0, prev(s) % tps, 0))],
        out_shape=[jax.ShapeDtypeStruct((t, MEM_WIDTH), F32),
                   jax.ShapeDtypeStruct((bsz, MLA_HEADS, seq, HEAD_PAD), BF16)],
        scratch_shapes=[pltpu.VMEM((TM, MLA_HEADS * HEAD_PAD), F32),
                        pltpu.VMEM((TM, MLA_HEADS * HEAD_PAD), F32)],
        compiler_params=_params(("arbitrary",)),
        name="proj_mla_q",
    )(h, w_b_pad, latent_gain_pad, w_uq_pad, q_gain_pad, cmul, smul)


def _flash_body(qi_tab, ki_tab, last_tab, q_ref, k_ref, v_ref, o_ref, m_scr, acc_scr, ea_scr, eb_scr, aa_scr, ab_scr):
    s = pl.program_id(1)
    npairs = pl.num_programs(1) - 1
    cur = jnp.minimum(s, npairs - 1)
    prv = jnp.maximum(s - 1, 0)
    diag = qi_tab[cur] == ki_tab[cur]
    reps = TQ // LANES

    @pl.when(s == 0)
    def _():
        acc_scr[...] = jnp.zeros(acc_scr.shape, F32)
        eb_scr[...] = jnp.zeros(eb_scr.shape, BF16)
        ab_scr[...] = jnp.zeros(ab_scr.shape, F32)

    def accumulate(h, e_old, a_old):
        acc_scr[h] = _lane_tile(a_old[h], V_EXT // LANES) * acc_scr[h] + _dot(e_old[h], v_ref[0, h])

    def step_diag(e_new, a_new, e_old, a_old):
        r_i = lax.broadcasted_iota(jnp.int32, (TQ, TQ), 0)
        c_i = lax.broadcasted_iota(jnp.int32, (TQ, TQ), 1)
        keep = c_i <= r_i
        for h in range(MLA_HEADS):
            sc = jnp.where(keep, _dot_nt(q_ref[0, h], k_ref[0, h]), -jnp.inf)
            m = jnp.broadcast_to(jnp.max(sc, axis=-1, keepdims=True), (TQ, LANES))
            a_new[h] = jnp.zeros((TQ, LANES), F32)
            e_new[h] = jnp.exp2(sc - _lane_tile(m, reps)).astype(BF16)
            m_scr[h] = m
            accumulate(h, e_old, a_old)

    def step_off(e_new, a_new, e_old, a_old):
        for h in range(MLA_HEADS):
            sc = _dot_nt(q_ref[0, h], k_ref[0, h])
            m_prev = m_scr[h]
            m = jnp.maximum(m_prev, jnp.max(sc, axis=-1, keepdims=True))
            a_new[h] = jnp.exp2(m_prev - m)
            e_new[h] = jnp.exp2(sc - _lane_tile(m, reps)).astype(BF16)
            m_scr[h] = m
            accumulate(h, e_old, a_old)

    even = s % 2 == 0
    for parity, bufs in ((even, (ea_scr, aa_scr, eb_scr, ab_scr)),
                         (jnp.logical_not(even), (eb_scr, ab_scr, ea_scr, aa_scr))):
        pl.when(jnp.logical_and(parity, diag))(functools.partial(step_diag, *bufs))
        pl.when(jnp.logical_and(parity, jnp.logical_not(diag)))(functools.partial(step_off, *bufs))

    @pl.when(jnp.logical_and(s >= 1, last_tab[prv] == 1))
    def _():
        for h in range(MLA_HEADS):
            acc = acc_scr[h]
            o_ref[:, h * MLA_V_DIM:(h + 1) * MLA_V_DIM] = (
                acc[:, :MLA_V_DIM] * (1.0 / acc[:, MLA_V_DIM:])).astype(o_ref.dtype)


def _flash(q, k, v):
    bsz, nh, seq, _ = q.shape
    nq = seq // TQ
    qi, ki, last = [], [], []
    for a in range(nq):
        order = [a] + list(range(a))
        for n, b in enumerate(order):
            qi.append(a)
            ki.append(b)
            last.append(1 if n == len(order) - 1 else 0)
    npairs = len(qi)
    tabs = [jnp.asarray(x, jnp.int32) for x in (qi, ki, last)]
    cur = lambda s: jnp.minimum(s, npairs - 1)
    prv = lambda s: jnp.maximum(s - 1, 0)
    grid_spec = pltpu.PrefetchScalarGridSpec(
        num_scalar_prefetch=3,
        grid=(bsz, npairs + 1),
        in_specs=[pl.BlockSpec((1, nh, TQ, HEAD_PAD), lambda b, s, qt, kt, lt: (b, 0, qt[cur(s)], 0)),
                  pl.BlockSpec((1, nh, TQ, HEAD_PAD), lambda b, s, qt, kt, lt: (b, 0, kt[cur(s)], 0)),
                  pl.BlockSpec((1, nh, TQ, V_EXT), lambda b, s, qt, kt, lt: (b, 0, kt[prv(s)], 0))],
        out_specs=pl.BlockSpec((TQ, nh * MLA_V_DIM), lambda b, s, qt, kt, lt: (b * nq + qt[prv(s)], 0)),
        scratch_shapes=[pltpu.VMEM((nh, TQ, LANES), F32),
                        pltpu.VMEM((nh, TQ, V_EXT), F32),
                        pltpu.VMEM((nh, TQ, TQ), BF16),
                        pltpu.VMEM((nh, TQ, TQ), BF16),
                        pltpu.VMEM((nh, TQ, LANES), F32),
                        pltpu.VMEM((nh, TQ, LANES), F32)],
    )
    return pl.pallas_call(
        _flash_body,
        grid_spec=grid_spec,
        out_shape=jax.ShapeDtypeStruct((bsz * seq, nh * MLA_V_DIM), BF16),
        compiler_params=_params(("parallel", "arbitrary")),
        name="mla_attention",
    )(*tabs, q, k, v)


def _pad_heads(w, nh, dh, dh_pad):
    lead = w.shape[:-1]
    w = w.reshape(lead + (nh, dh))
    w = jnp.pad(w, [(0, 0)] * len(lead) + [(0, 0), (0, dh_pad - dh)])
    return w.reshape(lead + (nh * dh_pad,))


def _pad_last(w, n):
    return jnp.pad(w, [(0, 0)] * (w.ndim - 1) + [(0, n - w.shape[-1])])


def kernel(x, mem, positions, ffn1_gain, ffn1_w_in, ffn1_w_out, mix_gain, w_out, mem_gain, w_mem_kv, mem_q_gain, mem_k_gain, a_w_in, a_b_gates, a_conv, a_head_gain, kv_gain, w_dkv, kv_latent_gain, w_ukv, k_gain, b_w_in, b_q_latent_gain, b_w_uq, b_q_gain, ffn2_gain, ffn2_w_in, ffn2_w_out):
    bsz, seq, d = x.shape
    t = bsz * seq
    assert d == D_MODEL and seq % TM == 0 and seq % CHUNK == 0 and seq % TQ == 0 and t % min(TROPE, t) == 0 and t % TM_FFN == 0

    ffn_w = (ffn1_w_in[0].astype(BF16), ffn1_w_out[0].astype(BF16))
    w_out_b = w_out.astype(BF16)

    nqk = MLSTM_HEADS * MLSTM_QK_DIM
    c0, c1, c2, c3 = 2 * nqk, 2 * nqk + MAIN_WIDTH, 2 * nqk + 2 * MAIN_WIDTH, 2 * nqk + 2 * MAIN_WIDTH + 2 * MLSTM_HEADS

    cmul, smul = _rope_tables(positions)
    mk, mv = _mem_kv(mem.reshape(bsz * N_MEM, d), mem_gain, w_mem_kv.astype(BF16), mem_k_gain)

    xs = x.reshape(t, d)
    k_sh = v_sh = None
    for layer in range(DEPTH):
        xs, hmix, ffn_w = _ffn(xs, ffn1_gain[layer], *ffn_w, next_gain=mix_gain[layer],
                               next_weights=(ffn2_w_in, ffn2_w_out, layer))
        if layer < N_A_LAYERS:
            w_in = a_w_in[layer]
            w_qk = jnp.concatenate([_pad_heads(w_in[:, :nqk], MLSTM_HEADS, MLSTM_QK_DIM, QK_PAD),
                                    _pad_heads(w_in[:, nqk:c0], MLSTM_HEADS, MLSTM_QK_DIM, QK_PAD)], axis=-1)
            conv_w = jnp.concatenate([_pad_heads(a_conv[layer][:, :nqk], MLSTM_HEADS, MLSTM_QK_DIM, QK_PAD),
                                      _pad_heads(a_conv[layer][:, nqk:], MLSTM_HEADS, MLSTM_QK_DIM, QK_PAD)], axis=-1)
            w_g = w_in[:, c2:c3]
            qk, v, og, mq, g, gt = _proj_a(
                hmix, w_qk.astype(BF16), w_in[:, c0:c1].astype(BF16), w_in[:, c1:c2].astype(BF16),
                w_in[:, c3:].astype(BF16), _pad_last(w_g, GATE_PAD).astype(BF16),
                jnp.pad(w_g.T, ((0, BF16_ROWS - 2 * MLSTM_HEADS), (0, 0))).astype(BF16), a_b_gates[layer])
            main = _mlstm(qk, conv_w, v, og, g, gt, a_head_gain[layer], bsz)
        else:
            j = layer - N_A_LAYERS
            w_in = b_w_in[j]
            w_b = jnp.concatenate([_pad_last(w_in[:, :Q_LORA_RANK], Q_LORA_PAD), w_in[:, Q_LORA_RANK:]], axis=-1)
            w_uq = jnp.pad(_pad_heads(b_w_uq[j], MLA_HEADS, MLA_QK_DIM, HEAD_PAD),
                           ((0, Q_LORA_PAD - Q_LORA_RANK), (0, 0))).astype(BF16)
            mq, q = _mla_q(hmix, w_b.astype(BF16), _pad_last(b_q_latent_gain[j], Q_LORA_PAD).reshape(1, Q_LORA_PAD),
                           w_uq, _pad_last(b_q_gain[j], HEAD_PAD).reshape(1, HEAD_PAD), cmul, smul, bsz)
            main = _flash(q, k_sh, v_sh)
        xs = _mixer_out(xs, main, mq, mk, mv, mem_q_gain.reshape(DEPTH, 1, MEM_HEAD_DIM), w_out_b, layer, seq)
        following = (ffn1_w_in, ffn1_w_out, layer + 1) if layer + 1 < DEPTH else None
        xs, hkv, ffn_w = _ffn(xs, ffn2_gain[layer], *ffn_w, next_gain=kv_gain if layer == N_A_LAYERS - 1 else None,
                              next_weights=following)
        if layer == N_A_LAYERS - 1:
            k_sh, v_sh = _shared_kv(hkv, _pad_last(w_dkv, KV_DOWN_PAD).astype(BF16),
                                    kv_latent_gain.reshape(1, KV_LORA_RANK), w_ukv.astype(BF16),
                                    k_gain[:MLA_NOPE_DIM].reshape(1, MLA_NOPE_DIM),
                                    _pad_last(k_gain[MLA_NOPE_DIM:], LANES).reshape(1, LANES), cmul, smul, bsz)
    return xs.reshape(bsz, seq, d)
```
